```python
import math
import jax, jax.numpy as jnp
from jax import lax
import numpy as np

D_MODEL = 1024
BATCH = 8
SEQ = 2048
DEPTH = 2
DEC_BATCH = 32
DEC_SEQ = 1
PAST_LEN = 16384
PAGE_SIZE = 128

N_MIXERS = 2
N_CONV_LAYERS = (DEPTH + 1) // 2
N_ATTN_LAYERS = DEPTH // 2
D_CONV = D_MODEL
CONV_WIDTH = 3
N_HEADS = 8
HEAD_DIM = D_MODEL // (2 * N_HEADS)
V_DIM = 2 * HEAD_DIM
D_ATTN = N_HEADS * V_DIM
ROT_DIM = HEAD_DIM // 4
ROPE_THETA = 500000.0
Q_BLOCK = 128
LN_EPS = 1e-5
RMS_EPS = 1e-5
ALPHA = (2 * DEPTH) ** 0.25
BETA = (8 * DEPTH) ** -0.25
NEG_INF = -1e30

kernel_name = "hybrid_shortconv_diffattn_step"


def layer_norm(x, g, b):
    xf = x.astype(jnp.float32)
    mu = jnp.mean(xf, axis=-1, keepdims=True)
    var = jnp.mean(jnp.square(xf - mu), axis=-1, keepdims=True)
    y = (xf - mu) * lax.rsqrt(var + LN_EPS) * g.astype(jnp.float32) + b.astype(jnp.float32)
    return y.astype(x.dtype)


def rope_partial(x, pos):
    half = ROT_DIM // 2
    inv_freq = jnp.power(ROPE_THETA, -jnp.arange(half, dtype=jnp.float32) * (2.0 / ROT_DIM))
    ang = pos.astype(jnp.float32)[:, None] * inv_freq[None, :]
    cos = jnp.cos(ang)[:, None, None, :]
    sin = jnp.sin(ang)[:, None, None, :]
    xr = x[..., :ROT_DIM].astype(jnp.float32)
    x1, x2 = xr[..., :half], xr[..., half:]
    rot = jnp.concatenate([x1 * cos - x2 * sin, x2 * cos + x1 * sin], axis=-1)
    return jnp.concatenate([rot.astype(x.dtype), x[..., ROT_DIM:]], axis=-1)


def conv_mixer(x, past, w_in, w_conv, w_out):
    t = x.shape[1]
    proj = x @ w_in
    b, c, v, z = jnp.split(proj, 4, axis=-1)
    u = c * v
    u_all = jnp.concatenate([past.astype(u.dtype), u], axis=1)
    conv = sum(w_conv[j] * u_all[:, j:j + t] for j in range(CONV_WIDTH))
    y = (jax.nn.silu(z) * b * conv) @ w_out
    return y, u_all[:, -(CONV_WIDTH - 1):]


def attn_project(x, pos, w_in):
    bsz, t, _ = x.shape
    proj = x @ w_in
    q, k, v, z = jnp.split(proj, 4, axis=-1)
    q = rope_partial(q.reshape(bsz, t, N_HEADS, 2, HEAD_DIM), pos)
    k = rope_partial(k.reshape(bsz, t, N_HEADS, 2, HEAD_DIM), pos)
    v = v.reshape(bsz, t, N_HEADS, V_DIM)
    return q, k, v, z


def lambda_full(lq1, lk1, lq2, lk2, lam_init):
    f = jnp.float32
    return (jnp.exp(jnp.sum(lq1.astype(f) * lk1.astype(f))) -
            jnp.exp(jnp.sum(lq2.astype(f) * lk2.astype(f))) + lam_init)


def diff_attn_prompt(q, k, v, lam):
    bsz, s = q.shape[:2]
    nb = s // Q_BLOCK
    qf = q.astype(jnp.float32) * (HEAD_DIM ** -0.5)
    kf = k.astype(jnp.float32)
    vf = v.astype(jnp.float32)
    qb = qf.reshape(bsz, nb, Q_BLOCK, N_HEADS, 2, HEAD_DIM).transpose(1, 0, 2, 3, 4, 5)
    kpos = jnp.arange(s)

    def block(args):
        qi, i = args
        sc = jnp.einsum('bqhcd,bkhcd->bhcqk', qi, kf)
        qpos = i * Q_BLOCK + jnp.arange(Q_BLOCK)
        mask = kpos[None, :] <= qpos[:, None]
        p = jax.nn.softmax(jnp.where(mask, sc, NEG_INF), axis=-1)
        diff = p[:, :, 0] - lam * p[:, :, 1]
        return jnp.einsum('bhqk,bkhv->bqhv', diff, vf)

    out = lax.map(block, (qb, jnp.arange(nb)))
    return out.transpose(1, 0, 2, 3, 4).reshape(bsz, s, N_HEADS, V_DIM)


def _online_update(carry, sc, vblk):
    m, l, acc = carry
    m_new = jnp.maximum(m, jnp.max(sc, axis=-1))
    corr = jnp.exp(m - m_new)
    p = jnp.exp(sc - m_new[..., None])
    l = l * corr + jnp.sum(p, axis=-1)
    acc = acc * corr[..., None] + jnp.einsum('bhctp,bphv->bhctv', p, vblk.astype(jnp.float32))
    return (m_new, l, acc)


def diff_attn_sample(q, k_new, v_new, cache_k, cache_v, page_table, lam):
    bsz, t = q.shape[:2]
    qf = q.astype(jnp.float32) * (HEAD_DIM ** -0.5)

    def page_step(carry, pidx):
        kblk = cache_k[pidx]
        vblk = cache_v[pidx]
        sc = jnp.einsum('bthcd,bphcd->bhctp', qf, kblk.astype(jnp.float32))
        return _online_update(carry, sc, vblk), None

    init = (jnp.full((bsz, N_HEADS, 2, t), NEG_INF, jnp.float32),
            jnp.zeros((bsz, N_HEADS, 2, t), jnp.float32),
            jnp.zeros((bsz, N_HEADS, 2, t, V_DIM), jnp.float32))
    carry, _ = lax.scan(page_step, init, page_table.T)
    sc_self = jnp.einsum('bthcd,buhcd->bhctu', qf, k_new.astype(jnp.float32))
    causal = jnp.arange(t)[:, None] >= jnp.arange(t)[None, :]
    sc_self = jnp.where(causal, sc_self, NEG_INF)
    m, l, acc = _online_update(carry, sc_self, v_new)
    o = acc / l[..., None]
    diff = o[:, :, 0] - lam * o[:, :, 1]
    return diff.transpose(0, 2, 1, 3)


def attn_output(o, z, subln_g, lam_init, w_out):
    bsz, t = o.shape[:2]
    of = o * lax.rsqrt(jnp.mean(jnp.square(o), axis=-1, keepdims=True) + RMS_EPS)
    of = of * subln_g.astype(jnp.float32) * (1.0 - lam_init)
    h = of.reshape(bsz, t, D_ATTN).astype(z.dtype) * jax.nn.silu(z)
    return h @ w_out


def setup_inputs(seed: int = 0) -> dict:
    key = jax.random.key(seed)
    ks = jax.random.split(key, 24)
    n_pages = PAST_LEN // PAGE_SIZE
    n_used = DEC_BATCH * n_pages
    n_pool = (5 * n_used) // 4
    nrm = jax.random.normal
    f = jnp.float32
    x_prompt = nrm(ks[0], (BATCH, SEQ, D_MODEL), f)
    x_sample = nrm(ks[1], (DEC_BATCH, DEC_SEQ, D_MODEL), f)
    state_conv = nrm(ks[2], (N_CONV_LAYERS, DEC_BATCH, CONV_WIDTH - 1, D_CONV), f)
    cache_k = nrm(ks[3], (N_ATTN_LAYERS, n_pool, PAGE_SIZE, N_HEADS, 2, HEAD_DIM), f)
    cache_v = nrm(ks[4], (N_ATTN_LAYERS, n_pool, PAGE_SIZE, N_HEADS, V_DIM), f)
    page_table = jax.random.permutation(ks[5], n_pool)[:n_used].reshape(DEC_BATCH, n_pages).astype(jnp.int32)
    conv_w_in = nrm(ks[6], (N_CONV_LAYERS, D_MODEL, 4 * D_CONV), f) * D_MODEL ** -0.5
    conv_w = nrm(ks[7], (N_CONV_LAYERS, CONV_WIDTH, D_CONV), f) * CONV_WIDTH ** -0.5
    conv_w_out = nrm(ks[8], (N_CONV_LAYERS, D_CONV, D_MODEL), f) * (D_CONV ** -0.5 * BETA)
    ln_conv_g = 1.0 + 0.02 * nrm(ks[9], (N_CONV_LAYERS, D_MODEL), f)
    ln_conv_b = 0.02 * nrm(ks[10], (N_CONV_LAYERS, D_MODEL), f)
    attn_w_in = nrm(ks[11], (N_ATTN_LAYERS, D_MODEL, 3 * 2 * N_HEADS * HEAD_DIM + D_ATTN), f) * D_MODEL ** -0.5
    lambda_q1 = 0.1 * nrm(ks[12], (N_ATTN_LAYERS, HEAD_DIM), f)
    lambda_k1 = 0.1 * nrm(ks[13], (N_ATTN_LAYERS, HEAD_DIM), f)
    lambda_q2 = 0.1 * nrm(ks[14], (N_ATTN_LAYERS, HEAD_DIM), f)
    lambda_k2 = 0.1 * nrm(ks[15], (N_ATTN_LAYERS, HEAD_DIM), f)
    subln_g = 1.0 + 0.02 * nrm(ks[16], (N_ATTN_LAYERS, V_DIM), f)
    attn_w_out = nrm(ks[17], (N_ATTN_LAYERS, D_ATTN, D_MODEL), f) * (D_ATTN ** -0.5 * BETA)
    ln_attn_g = 1.0 + 0.02 * nrm(ks[18], (N_ATTN_LAYERS, D_MODEL), f)
    ln_attn_b = 0.02 * nrm(ks[19], (N_ATTN_LAYERS, D_MODEL), f)
    return {"x_prompt": x_prompt, "x_sample": x_sample, "state_conv": state_conv,
            "cache_k": cache_k, "cache_v": cache_v, "page_table": page_table,
            "conv_w_in": conv_w_in, "conv_w": conv_w, "conv_w_out": conv_w_out,
            "ln_conv_g": ln_conv_g, "ln_conv_b": ln_conv_b,
            "attn_w_in": attn_w_in, "lambda_q1": lambda_q1, "lambda_k1": lambda_k1,
            "lambda_q2": lambda_q2, "lambda_k2": lambda_k2, "subln_g": subln_g,
            "attn_w_out": attn_w_out, "ln_attn_g": ln_attn_g, "ln_attn_b": ln_attn_b}


def reference(x_prompt, x_sample, state_conv, cache_k, cache_v, page_table,
              conv_w_in, conv_w, conv_w_out, ln_conv_g, ln_conv_b,
              attn_w_in, lambda_q1, lambda_k1, lambda_q2, lambda_k2, subln_g,
              attn_w_out, ln_attn_g, ln_attn_b):
    yp, ys = x_prompt, x_sample
    pos_p = jnp.arange(yp.shape[1])
    pos_s = PAST_LEN + jnp.arange(ys.shape[1])
    conv_p, conv_s, kp_l, vp_l, ks_l, vs_l = [], [], [], [], [], []
    for layer in range(DEPTH):
        i = layer // N_MIXERS
        if layer % N_MIXERS == 0:
            zero_past = jnp.zeros((yp.shape[0], CONV_WIDTH - 1, D_CONV), yp.dtype)
            hp, sp = conv_mixer(yp, zero_past, conv_w_in[i], conv_w[i], conv_w_out[i])
            hs, ss = conv_mixer(ys, state_conv[i], conv_w_in[i], conv_w[i], conv_w_out[i])
            yp = layer_norm(ALPHA * yp + hp, ln_conv_g[i], ln_conv_b[i])
            ys = layer_norm(ALPHA * ys + hs, ln_conv_g[i], ln_conv_b[i])
            conv_p.append(sp)
            conv_s.append(ss)
        else:
            lam_init = 0.8 - 0.6 * math.exp(-0.3 * layer)
            lam = lambda_full(lambda_q1[i], lambda_k1[i], lambda_q2[i], lambda_k2[i], lam_init)
            qp, kp, vp, zp = attn_project(yp, pos_p, attn_w_in[i])
            qs, kss, vss, zs = attn_project(ys, pos_s, attn_w_in[i])
            op = diff_attn_prompt(qp, kp, vp, lam)
            osm = diff_attn_sample(qs, kss, vss, cache_k[i], cache_v[i], page_table, lam)
            hp = attn_output(op, zp, subln_g[i], lam_init, attn_w_out[i])
            hs = attn_output(osm, zs, subln_g[i], lam_init, attn_w_out[i])
            yp = layer_norm(ALPHA * yp + hp, ln_attn_g[i], ln_attn_b[i])
            ys = layer_norm(ALPHA * ys + hs, ln_attn_g[i], ln_attn_b[i])
            kp_l.append(kp)
            vp_l.append(vp)
            ks_l.append(kss)
            vs_l.append(vss)
    conv_state_prompt = jnp.stack(conv_p)
    conv_state_sample = jnp.stack(conv_s)
    k_prompt = jnp.stack(kp_l)
    v_prompt = jnp.stack(vp_l)
    k_sample = jnp.stack(ks_l)
    v_sample = jnp.stack(vs_l)
    return (yp, ys, conv_state_prompt, conv_state_sample, k_prompt, v_prompt, k_sample, v_sample)
```

```python
import math

import jax
import jax.numpy as jnp
from jax import lax
from jax.experimental import pallas as pl
from jax.experimental.pallas import tpu as pltpu

D_MODEL = 1024
DEPTH = 2
PAST_LEN = 16384
PAGE_SIZE = 128
CONV_WIDTH = 3
N_HEADS = 8
HEAD_DIM = D_MODEL // (2 * N_HEADS)
V_DIM = 2 * HEAD_DIM
ROT_DIM = HEAD_DIM // 4
ROT_HALF = ROT_DIM // 2
ROPE_THETA = 500000.0
LN_EPS = 1e-5
RMS_EPS = 1e-5
ALPHA = (2 * DEPTH) ** 0.25
NEG_INF = -1e30
LAM_INIT = 0.8 - 0.6 * math.exp(-0.3 * 1)

LANES = 128
SUBLANES = 8
VMEM_LIMIT_BYTES = 56 * 1024 * 1024

CONV_TILE = 512
ATTN_TILE = 512
ATTN_SUB = 256
PAGES_PER_STEP = 8

F32 = jnp.float32
BF16 = jnp.bfloat16

assert ROT_HALF == SUBLANES


def _dot(a, b):
    return jnp.dot(a, b, preferred_element_type=F32)


def _dot_nt(a, b):
    return lax.dot_general(a, b, (((1,), (1,)), ((), ())), preferred_element_type=F32)


def _layer_norm(x, g, b):
    mu = jnp.mean(x, axis=-1, keepdims=True)
    xc = x - mu
    var = jnp.mean(xc * xc, axis=-1, keepdims=True)
    return xc * lax.rsqrt(var + LN_EPS) * g + b


def _silu(z):
    return z * (1.0 / (1.0 + jnp.exp(-z)))


def _rope_head(x, cos_t, sin_a, sin_b):
    up = pltpu.roll(x, LANES - ROT_HALF, axis=1)
    dn = pltpu.roll(x, ROT_HALF, axis=1)
    return x * cos_t + up * sin_a + dn * sin_b


def _lambda_full(lamv):
    a = jnp.sum(lamv[0:1, :] * lamv[1:2, :], axis=-1, keepdims=True)
    b = jnp.sum(lamv[2:3, :] * lamv[3:4, :], axis=-1, keepdims=True)
    return jnp.exp(a) - jnp.exp(b) + LAM_INIT


def _subln_gate(diff, subln_g):
    ms = jnp.mean(diff * diff, axis=-1, keepdims=True)
    return diff * lax.rsqrt(ms + RMS_EPS) * subln_g * (1.0 - LAM_INIT)


def _rope_angles(pos):
    inv_freq = jnp.power(ROPE_THETA, -jnp.arange(ROT_HALF, dtype=F32) * (2.0 / ROT_DIM))
    ang = pos.astype(F32)[:, None] * inv_freq[None, :]
    return jnp.cos(ang), jnp.sin(ang)


def _rope_tables(pos):
    cos, sin = _rope_angles(pos)
    rows = pos.shape[0]
    ones = jnp.ones((rows, HEAD_DIM - ROT_DIM), F32)
    zeros = jnp.zeros((rows, HEAD_DIM - ROT_DIM), F32)
    zh = jnp.zeros((rows, ROT_HALF), F32)
    cos64 = jnp.concatenate([cos, cos, ones], axis=1)
    sa64 = jnp.concatenate([-sin, zh, zeros], axis=1)
    sb64 = jnp.concatenate([zh, sin, zeros], axis=1)
    tile2 = lambda t: jnp.concatenate([t, t], axis=1)
    return tile2(cos64), tile2(sa64), tile2(sb64)


def _conv_prompt_kernel(x_ref, w_in_ref, w_conv_ref, w_out_ref, g_ref, b_ref,
                        y_ref, state_ref, ubuf_ref):
    t = x_ref.shape[1]
    pad = SUBLANES

    @pl.when(pl.program_id(1) == 0)
    def _():
        ubuf_ref[0:pad, :] = jnp.zeros((pad, D_MODEL), F32)

    x = x_ref[0]
    xb = x.astype(BF16)
    c = _dot(xb, w_in_ref[:, D_MODEL:2 * D_MODEL])
    v = _dot(xb, w_in_ref[:, 2 * D_MODEL:3 * D_MODEL])
    u = c * v
    ubuf_ref[pad:pad + t, :] = u
    conv = (w_conv_ref[0:1, :] * ubuf_ref[pad - 2:pad - 2 + t, :]
            + w_conv_ref[1:2, :] * ubuf_ref[pad - 1:pad - 1 + t, :]
            + w_conv_ref[2:3, :] * u)
    last2 = u[t - 2:t, :]
    ubuf_ref[pad - 2:pad, :] = last2
    state_ref[0] = last2
    bgate = _dot(xb, w_in_ref[:, 0:D_MODEL])
    z = _dot(xb, w_in_ref[:, 3 * D_MODEL:4 * D_MODEL])
    h = _dot((_silu(z) * bgate * conv).astype(BF16), w_out_ref[...])
    y_ref[0] = _layer_norm(ALPHA * x + h, g_ref[...], b_ref[...])


def _conv_prompt(x, w_in, w_conv, w_out, g, b):
    bsz, s, _ = x.shape
    t = CONV_TILE
    const = lambda *_: (0, 0)
    return pl.pallas_call(
        _conv_prompt_kernel,
        grid=(bsz, s // t),
        in_specs=[
            pl.BlockSpec((1, t, D_MODEL), lambda i, j: (i, j, 0)),
            pl.BlockSpec((D_MODEL, 4 * D_MODEL), const, pipeline_mode=pl.Buffered(1)),
            pl.BlockSpec((CONV_WIDTH, D_MODEL), const),
            pl.BlockSpec((D_MODEL, D_MODEL), const, pipeline_mode=pl.Buffered(1)),
            pl.BlockSpec((1, D_MODEL), const),
            pl.BlockSpec((1, D_MODEL), const),
        ],
        out_specs=[
            pl.BlockSpec((1, t, D_MODEL), lambda i, j: (i, j, 0)),
            pl.BlockSpec((1, CONV_WIDTH - 1, D_MODEL), lambda i, j: (i, 0, 0)),
        ],
        out_shape=[
            jax.ShapeDtypeStruct((bsz, s, D_MODEL), F32),
            jax.ShapeDtypeStruct((bsz, CONV_WIDTH - 1, D_MODEL), F32),
        ],
        scratch_shapes=[pltpu.VMEM((t + SUBLANES, D_MODEL), F32)],
        compiler_params=pltpu.CompilerParams(
            dimension_semantics=("arbitrary", "arbitrary"),
            vmem_limit_bytes=VMEM_LIMIT_BYTES),
        name="conv_prompt",
    )(x, w_in, w_conv, w_out, g, b)


def _attn_prompt_kernel(x_ref, w_qvz_ref, w_kt_ref, w_out_ref, cos_ref, sa_ref, sb_ref,
                        cos_t_ref, sin_t_ref, lamv_ref, subln_ref, g_ref, b_ref,
                        kt_out_ref, v_out_ref, y_ref,
                        kc_ref, vc_ref, qs_ref, of_ref, m_ref, l_ref, acc_ref):
    t = x_ref.shape[1]
    sub = ATTN_SUB
    n_sub = t // sub
    qi = pl.program_id(1)

    x = x_ref[0]
    xb = x.astype(BF16)

    kt = _dot_nt(w_kt_ref[...], xb)
    cos_c, sin_c = cos_t_ref[...], sin_t_ref[...]
    for g in range(2 * N_HEADS):
        r0 = g * HEAD_DIM
        x1 = kt[r0:r0 + ROT_HALF, :]
        x2 = kt[r0 + ROT_HALF:r0 + ROT_DIM, :]
        kt_out_ref[0, r0:r0 + ROT_HALF, :] = x1 * cos_c - x2 * sin_c
        kt_out_ref[0, r0 + ROT_HALF:r0 + ROT_DIM, :] = x2 * cos_c + x1 * sin_c
        kt_out_ref[0, r0 + ROT_DIM:r0 + HEAD_DIM, :] = kt[r0 + ROT_DIM:r0 + HEAD_DIM, :]
    for h in range(N_HEADS):
        for a in range(n_sub):
            kc_ref[h, qi * n_sub + a] = kt_out_ref[0, h * LANES:(h + 1) * LANES,
                                                   a * sub:(a + 1) * sub].astype(BF16)

    v = _dot(xb, w_qvz_ref[:, D_MODEL:2 * D_MODEL])
    for h in range(N_HEADS):
        vh = v[:, h * LANES:(h + 1) * LANES]
        v_out_ref[0, pl.ds(h, t, stride=N_HEADS), :] = vh
        for a in range(n_sub):
            vc_ref[h, qi * n_sub + a] = vh[a * sub:(a + 1) * sub, :].astype(BF16)

    q = _dot(xb, w_qvz_ref[:, 0:D_MODEL])
    cos_r, sin_a, sin_b = cos_ref[...], sa_ref[...], sb_ref[...]
    first_map = lax.broadcasted_iota(jnp.int32, (t, LANES), 1) < HEAD_DIM
    for h in range(N_HEADS):
        qh = _rope_head(q[:, h * LANES:(h + 1) * LANES], cos_r, sin_a, sin_b) * (HEAD_DIM ** -0.5)
        q1 = jnp.where(first_map, qh, 0.0).astype(BF16)
        q2 = jnp.where(first_map, 0.0, qh).astype(BF16)
        for a in range(n_sub):
            rs = slice(a * sub, (a + 1) * sub)
            qs_ref[h * n_sub + a, 0:sub, :] = q1[rs]
            qs_ref[h * n_sub + a, sub:2 * sub, :] = q2[rs]

    lam = _lambda_full(lamv_ref[...])
    subln_g = subln_ref[...]
    rows2 = lax.broadcasted_iota(jnp.int32, (2 * sub, sub), 0)
    cols2 = lax.broadcasted_iota(jnp.int32, (2 * sub, sub), 1)
    causal = cols2 <= jnp.where(rows2 >= sub, rows2 - sub, rows2)

    def block_update(lhs, h, kb, mask):
        s = _dot(lhs, kc_ref[h, kb])
        if mask is not None:
            s = jnp.where(mask, s, NEG_INF)
        m_prev = m_ref[...]
        m_new = jnp.maximum(m_prev, jnp.max(s, axis=-1, keepdims=True))
        corr = jnp.exp(m_prev - m_new)
        p = jnp.exp(s - m_new)
        l_ref[...] = l_ref[...] * corr + jnp.sum(p, axis=-1, keepdims=True)
        acc_ref[...] = acc_ref[...] * corr + _dot(p.astype(BF16), vc_ref[h, kb])
        m_ref[...] = m_new

    def head_body(h, carry):
        for a in range(n_sub):
            lhs = qs_ref[h * n_sub + a]
            m_ref[...] = jnp.full(m_ref.shape, NEG_INF, F32)
            l_ref[...] = jnp.zeros(l_ref.shape, F32)
            acc_ref[...] = jnp.zeros(acc_ref.shape, F32)
            n_full = qi * n_sub + a

            def kv_body(kb, c):
                block_update(lhs, h, kb, None)
                return c

            lax.fori_loop(0, n_full, kv_body, 0)
            block_update(lhs, h, n_full, causal)
            o = acc_ref[...] / l_ref[...]
            diff = o[0:sub, :] - lam * o[sub:2 * sub, :]
            of_ref[h, a * sub:(a + 1) * sub, :] = _subln_gate(diff, subln_g)
        return carry

    lax.fori_loop(0, N_HEADS, head_body, 0)

    z = _dot(xb, w_qvz_ref[:, 2 * D_MODEL:3 * D_MODEL])
    of = jnp.concatenate([of_ref[h] for h in range(N_HEADS)], axis=1)
    hout = _dot((of * _silu(z)).astype(BF16), w_out_ref[...])
    y_ref[0] = _layer_norm(ALPHA * x + hout, g_ref[...], b_ref[...])


def _attn_prompt(x, w_qvz, w_kt, w_out, tables, tables_t, lamv, subln_g, g, b):
    bsz, s, _ = x.shape
    t = ATTN_TILE
    sub = ATTN_SUB
    const = lambda *_: (0, 0)
    weight = lambda shape: pl.BlockSpec(shape, const, pipeline_mode=pl.Buffered(1))
    tile = pl.BlockSpec((1, t, D_MODEL), lambda i, j: (i, j, 0))
    table = pl.BlockSpec((t, LANES), lambda i, j: (j, 0))
    table_t = pl.BlockSpec((ROT_HALF, t), lambda i, j: (0, j))
    return pl.pallas_call(
        _attn_prompt_kernel,
        grid=(bsz, s // t),
        in_specs=[
            tile,
            weight((D_MODEL, 3 * D_MODEL)), weight((D_MODEL, D_MODEL)), weight((D_MODEL, D_MODEL)),
            table, table, table, table_t, table_t,
            pl.BlockSpec((4, HEAD_DIM), const),
            pl.BlockSpec((1, V_DIM), const),
            pl.BlockSpec((1, D_MODEL), const),
            pl.BlockSpec((1, D_MODEL), const),
        ],
        out_specs=[
            pl.BlockSpec((1, D_MODEL, t), lambda i, j: (i, 0, j)),
            pl.BlockSpec((1, t * N_HEADS, V_DIM), lambda i, j: (i, j, 0)),
            tile,
        ],
        out_shape=[
            jax.ShapeDtypeStruct((bsz, D_MODEL, s), F32),
            jax.ShapeDtypeStruct((bsz, s * N_HEADS, V_DIM), F32),
            jax.ShapeDtypeStruct((bsz, s, D_MODEL), F32),
        ],
        scratch_shapes=[
            pltpu.VMEM((N_HEADS, s // sub, LANES, sub), BF16),
            pltpu.VMEM((N_HEADS, s // sub, sub, LANES), BF16),
            pltpu.VMEM((N_HEADS * (t // sub), 2 * sub, LANES), BF16),
            pltpu.VMEM((N_HEADS, t, LANES), F32),
            pltpu.VMEM((2 * sub, 1), F32),
            pltpu.VMEM((2 * sub, 1), F32),
            pltpu.VMEM((2 * sub, LANES), F32),
        ],
        compiler_params=pltpu.CompilerParams(
            dimension_semantics=("arbitrary", "arbitrary"),
            vmem_limit_bytes=VMEM_LIMIT_BYTES),
        name="attn_prompt",
    )(x, w_qvz, w_kt, w_out, *tables, *tables_t, lamv, subln_g, g, b)


def _sample_pre_kernel(x_ref, past_ref, cw_in_ref, w_conv_ref, cw_out_ref, cg_ref, cb_ref,
                       aw_in_ref, cos_ref, sa_ref, sb_ref,
                       state_ref, y1_ref, q_ref, k_ref, v_ref, z_ref):
    x = x_ref[...]
    xb = x.astype(BF16)
    past0 = past_ref[:, 0:D_MODEL]
    past1 = past_ref[:, D_MODEL:2 * D_MODEL]
    bgate = _dot(xb, cw_in_ref[:, 0:D_MODEL])
    c = _dot(xb, cw_in_ref[:, D_MODEL:2 * D_MODEL])
    v = _dot(xb, cw_in_ref[:, 2 * D_MODEL:3 * D_MODEL])
    z = _dot(xb, cw_in_ref[:, 3 * D_MODEL:4 * D_MODEL])
    u = c * v
    conv = w_conv_ref[0:1, :] * past0 + w_conv_ref[1:2, :] * past1 + w_conv_ref[2:3, :] * u
    state_ref[:, 0:D_MODEL] = past1
    state_ref[:, D_MODEL:2 * D_MODEL] = u
    h = _dot((_silu(z) * bgate * conv).astype(BF16), cw_out_ref[...])
    y1 = _layer_norm(ALPHA * x + h, cg_ref[...], cb_ref[...])
    y1_ref[...] = y1

    yb = y1.astype(BF16)
    cos_t, sin_a, sin_b = cos_ref[...], sa_ref[...], sb_ref[...]
    q = _dot(yb, aw_in_ref[:, 0:D_MODEL])
    k = _dot(yb, aw_in_ref[:, D_MODEL:2 * D_MODEL])
    v_ref[...] = _dot(yb, aw_in_ref[:, 2 * D_MODEL:3 * D_MODEL])
    z_ref[...] = _dot(yb, aw_in_ref[:, 3 * D_MODEL:4 * D_MODEL])
    for h_i in range(N_HEADS):
        hs = slice(h_i * LANES, (h_i + 1) * LANES)
        k_ref[:, hs] = _rope_head(k[:, hs], cos_t, sin_a, sin_b)
        q_ref[:, hs] = _rope_head(q[:, hs], cos_t, sin_a, sin_b) * (HEAD_DIM ** -0.5)


def _sample_pre(x, past, cw_in, w_conv, cw_out, cg, cb, aw_in, tables):
    n = x.shape[0]
    full = lambda shape: pl.BlockSpec(shape, lambda i: (0,) * len(shape))
    weight = lambda shape: pl.BlockSpec(shape, lambda i: (0, 0), pipeline_mode=pl.Buffered(1))
    row = jax.ShapeDtypeStruct((n, D_MODEL), F32)
    return pl.pallas_call(
        _sample_pre_kernel,
        grid=(1,),
        in_specs=[
            full((n, D_MODEL)), full((n, 2 * D_MODEL)),
            weight((D_MODEL, 4 * D_MODEL)), full((CONV_WIDTH, D_MODEL)), weight((D_MODEL, D_MODEL)),
            full((1, D_MODEL)), full((1, D_MODEL)),
            weight((D_MODEL, 4 * D_MODEL)),
            full((1, LANES)), full((1, LANES)), full((1, LANES)),
        ],
        out_specs=[full((n, 2 * D_MODEL))] + [full((n, D_MODEL))] * 5,
        out_shape=[jax.ShapeDtypeStruct((n, 2 * D_MODEL), F32)] + [row] * 5,
        compiler_params=pltpu.CompilerParams(
            dimension_semantics=("arbitrary",), vmem_limit_bytes=VMEM_LIMIT_BYTES),
        name="sample_pre",
    )(x, past, cw_in, w_conv, cw_out, cg, cb, aw_in, *tables)


def _sample_decode_kernel(pt_ref, q_ref, knew_ref, vnew_ref, lamv_ref, *refs):
    del pt_ref
    n = PAGES_PER_STEP
    kt_refs, v_refs = refs[0:n], refs[n:2 * n]
    o_ref = refs[2 * n]
    qm_ref, m_ref, l_ref, acc_ref = refs[2 * n + 1:]
    j = pl.program_id(1)
    n_maps = 2 * N_HEADS
    rows = lax.broadcasted_iota(jnp.int32, (n_maps, D_MODEL), 0)
    lanes = lax.broadcasted_iota(jnp.int32, (n_maps, D_MODEL), 1)

    @pl.when(j == 0)
    def _():
        qm_ref[...] = jnp.where(lanes // HEAD_DIM == rows, q_ref[...], 0.0).astype(BF16)
        m_ref[...] = jnp.full(m_ref.shape, NEG_INF, F32)
        l_ref[...] = jnp.zeros(l_ref.shape, F32)
        acc_ref[...] = jnp.zeros(acc_ref.shape, F32)

    qm = qm_ref[...]
    s = jnp.concatenate([_dot(qm, kt_refs[i][...].astype(BF16)) for i in range(n)], axis=1)
    m_prev = m_ref[...]
    m_new = jnp.maximum(m_prev, jnp.max(s, axis=-1, keepdims=True))
    corr = jnp.exp(m_prev - m_new)
    p = jnp.exp(s - m_new)
    l_new = l_ref[...] * corr + jnp.sum(p, axis=-1, keepdims=True)
    pb = p.astype(BF16)

    def page_values(i):
        return jnp.concatenate(
            [v_refs[i][pl.ds(h, PAGE_SIZE, stride=N_HEADS), :] for h in range(N_HEADS)],
            axis=1).astype(BF16)

    pv = _dot(pb[:, 0:PAGE_SIZE], page_values(0))
    for i in range(1, n):
        pv += _dot(pb[:, i * PAGE_SIZE:(i + 1) * PAGE_SIZE], page_values(i))
    acc_new = acc_ref[...] * corr + pv
    m_ref[...] = m_new
    l_ref[...] = l_new
    acc_ref[...] = acc_new

    @pl.when(j == pl.num_programs(1) - 1)
    def _():
        knew = knew_ref[...].astype(BF16).astype(F32)
        vnew = vnew_ref[...].astype(BF16).astype(F32)
        s_self = jnp.sum(qm.astype(F32) * knew, axis=-1, keepdims=True)
        m_fin = jnp.maximum(m_new, s_self)
        corr_f = jnp.exp(m_new - m_fin)
        p_self = jnp.exp(s_self - m_fin)
        l_fin = l_new * corr_f + p_self
        acc_fin = acc_new * corr_f + p_self.astype(BF16).astype(F32) * vnew
        o = acc_fin / l_fin
        lam = _lambda_full(lamv_ref[...])
        own_head = lanes // V_DIM == rows // 2
        weight = jnp.where(rows % 2 == 0, 1.0, -lam)
        o_ref[...] = jnp.sum(jnp.where(own_head, o * weight, 0.0), axis=0, keepdims=True)


def _sample_decode(page_table, q, k_new, v_new, lamv, cache_kt, cache_v):
    n = q.shape[0]
    n_pages = page_table.shape[1]
    pps = PAGES_PER_STEP
    pt = page_table.reshape(-1)
    row = pl.BlockSpec((None, 1, D_MODEL), lambda b, j, pt_ref: (b, 0, 0))

    def page_spec(i, shape):
        return pl.BlockSpec((None,) + shape,
                            lambda b, j, pt_ref: (pt_ref[b * n_pages + j * pps + i], 0, 0))

    grid_spec = pltpu.PrefetchScalarGridSpec(
        num_scalar_prefetch=1,
        grid=(n, n_pages // pps),
        in_specs=[row, row, row, pl.BlockSpec((4, HEAD_DIM), lambda b, j, pt_ref: (0, 0))]
        + [page_spec(i, (D_MODEL, PAGE_SIZE)) for i in range(pps)]
        + [page_spec(i, (PAGE_SIZE * N_HEADS, V_DIM)) for i in range(pps)],
        out_specs=row,
        scratch_shapes=[
            pltpu.VMEM((2 * N_HEADS, D_MODEL), BF16),
            pltpu.VMEM((2 * N_HEADS, 1), F32),
            pltpu.VMEM((2 * N_HEADS, 1), F32),
            pltpu.VMEM((2 * N_HEADS, D_MODEL), F32),
        ],
    )
    r3 = lambda a: a.reshape(n, 1, D_MODEL)
    out = pl.pallas_call(
        _sample_decode_kernel,
        grid_spec=grid_spec,
        out_shape=jax.ShapeDtypeStruct((n, 1, D_MODEL), F32),
        compiler_params=pltpu.CompilerParams(
            dimension_semantics=("arbitrary", "arbitrary"),
            vmem_limit_bytes=VMEM_LIMIT_BYTES),
        name="sample_decode",
    )(pt, r3(q), r3(k_new), r3(v_new), lamv, *([cache_kt] * pps), *([cache_v] * pps))
    return out.reshape(n, D_MODEL)


def _sample_post_kernel(diff_ref, z_ref, y1_ref, subln_ref, w_out_ref, g_ref, b_ref, y_ref):
    subln_g = subln_ref[...]
    of = jnp.concatenate(
        [_subln_gate(diff_ref[:, h * LANES:(h + 1) * LANES], subln_g) for h in range(N_HEADS)], axis=1)
    hout = _dot((of * _silu(z_ref[...])).astype(BF16), w_out_ref[...])
    y_ref[...] = _layer_norm(ALPHA * y1_ref[...] + hout, g_ref[...], b_ref[...])


def _sample_post(diff, z, y1, subln_g, w_out, g, b):
    n = diff.shape[0]
    full = lambda shape: pl.BlockSpec(shape, lambda i: (0,) * len(shape))
    return pl.pallas_call(
        _sample_post_kernel,
        grid=(1,),
        in_specs=[full((n, D_MODEL))] * 3
        + [full((1, V_DIM)), full((D_MODEL, D_MODEL)), full((1, D_MODEL)), full((1, D_MODEL))],
        out_specs=full((n, D_MODEL)),
        out_shape=jax.ShapeDtypeStruct((n, D_MODEL), F32),
        compiler_params=pltpu.CompilerParams(dimension_semantics=("arbitrary",)),
        name="sample_post",
    )(diff, z, y1, subln_g, w_out, g, b)


def kernel(x_prompt, x_sample, state_conv, cache_k, cache_v, page_table, conv_w_in, conv_w, conv_w_out,
           ln_conv_g, ln_conv_b, attn_w_in, lambda_q1, lambda_k1, lambda_q2, lambda_k2, subln_g,
           attn_w_out, ln_attn_g, ln_attn_b):
    bsz, seq, _ = x_prompt.shape
    n_dec = x_sample.shape[0]
    pool = cache_k.shape[0] * cache_k.shape[1]

    cw_in = conv_w_in[0].astype(BF16)
    cw_out = conv_w_out[0].astype(BF16)
    aw_in = attn_w_in[0].astype(BF16)
    aw_qvz = jnp.concatenate([aw_in[:, 0:D_MODEL], aw_in[:, 2 * D_MODEL:4 * D_MODEL]], axis=1)
    aw_kt = aw_in[:, D_MODEL:2 * D_MODEL].T
    aw_out = attn_w_out[0].astype(BF16)
    w_conv = conv_w[0]
    cg, cb = ln_conv_g[0:1], ln_conv_b[0:1]
    ag, ab = ln_attn_g[0:1], ln_attn_b[0:1]
    lamv = jnp.concatenate(
        [lambda_q1[0:1], lambda_k1[0:1], lambda_q2[0:1], lambda_k2[0:1]], axis=0)
    sg = subln_g[0:1]
    pos_p = jnp.arange(seq)
    tables_p = _rope_tables(pos_p)
    tables_pt = tuple(a.T for a in _rope_angles(pos_p))
    tables_s = _rope_tables(PAST_LEN + jnp.arange(1))

    y1_p, conv_state_p = _conv_prompt(x_prompt, cw_in, w_conv, cw_out, cg, cb)
    kt_p, v_p, y_p = _attn_prompt(y1_p, aw_qvz, aw_kt, aw_out, tables_p, tables_pt, lamv, sg, ag, ab)
    k_p = kt_p.reshape(1, bsz, N_HEADS, 2, HEAD_DIM, seq).transpose(0, 1, 5, 2, 3, 4)

    past = state_conv[0].reshape(n_dec, (CONV_WIDTH - 1) * D_MODEL)
    state_s, y1_s, q_s, k_s, v_s, z_s = _sample_pre(
        x_sample.reshape(n_dec, D_MODEL), past, cw_in, w_conv, cw_out, cg, cb, aw_in, tables_s)
    cache_kt = cache_k.transpose(0, 1, 3, 4, 5, 2).reshape(pool, D_MODEL, PAGE_SIZE)
    cache_vr = cache_v.reshape(pool, PAGE_SIZE * N_HEADS, V_DIM)
    diff_s = _sample_decode(page_table, q_s, k_s, v_s, lamv, cache_kt, cache_vr)
    y_s = _sample_post(diff_s, z_s, y1_s, sg, aw_out, ag, ab)

    return (
        y_p,
        y_s.reshape(n_dec, 1, D_MODEL),
        conv_state_p[None],
        state_s.reshape(1, n_dec, CONV_WIDTH - 1, D_MODEL),
        k_p,
        v_p.reshape(1, bsz, seq, N_HEADS, V_DIM),
        k_s.reshape(1, n_dec, 1, N_HEADS, 2, HEAD_DIM),
        v_s.reshape(1, n_dec, 1, N_HEADS, V_DIM),
    )
```

```python
import math

import jax
import jax.numpy as jnp
from jax import lax
from jax.experimental import pallas as pl
from jax.experimental.pallas import tpu as pltpu

D_MODEL = 1024
DEPTH = 2
PAST_LEN = 16384
PAGE_SIZE = 128
CONV_WIDTH = 3
N_HEADS = 8
HEAD_DIM = D_MODEL // (2 * N_HEADS)
V_DIM = 2 * HEAD_DIM
ROT_DIM = HEAD_DIM // 4
ROT_HALF = ROT_DIM // 2
ROPE_THETA = 500000.0
LN_EPS = 1e-5
RMS_EPS = 1e-5
ALPHA = (2 * DEPTH) ** 0.25
NEG_INF = -1e30
LAM_INIT = 0.8 - 0.6 * math.exp(-0.3 * 1)
LOG2E = math.log2(math.e)

LANES = 128
SUBLANES = 8
VMEM_LIMIT_BYTES = 56 * 1024 * 1024

CONV_TILE = 512
ATTN_TILE = 512
ATTN_SUB = 256
PAGES_PER_STEP = 8

F32 = jnp.float32
BF16 = jnp.bfloat16

assert ROT_HALF == SUBLANES


def _dot(a, b):
    return jnp.dot(a, b, preferred_element_type=F32)


def _dot_nt(a, b):
    return lax.dot_general(a, b, (((1,), (1,)), ((), ())), preferred_element_type=F32)


def _layer_norm(x, g, b):
    mu = jnp.mean(x, axis=-1, keepdims=True)
    xc = x - mu
    var = jnp.mean(xc * xc, axis=-1, keepdims=True)
    return xc * lax.rsqrt(var + LN_EPS) * g + b


def _silu(z):
    return z * (1.0 / (1.0 + jnp.exp(-z)))


def _rope_head(x, cos_t, sin_a, sin_b):
    up = pltpu.roll(x, LANES - ROT_HALF, axis=1)
    dn = pltpu.roll(x, ROT_HALF, axis=1)
    return x * cos_t + up * sin_a + dn * sin_b


def _lambda_full(lamv):
    a = jnp.sum(lamv[0:1, :] * lamv[1:2, :], axis=-1, keepdims=True)
    b = jnp.sum(lamv[2:3, :] * lamv[3:4, :], axis=-1, keepdims=True)
    return jnp.exp(a) - jnp.exp(b) + LAM_INIT


def _subln_gate(diff, subln_g):
    ms = jnp.mean(diff * diff, axis=-1, keepdims=True)
    return diff * lax.rsqrt(ms + RMS_EPS) * subln_g * (1.0 - LAM_INIT)


def _rope_angles(pos):
    inv_freq = jnp.power(ROPE_THETA, -jnp.arange(ROT_HALF, dtype=F32) * (2.0 / ROT_DIM))
    ang = pos.astype(F32)[:, None] * inv_freq[None, :]
    return jnp.cos(ang), jnp.sin(ang)


def _rope_tables(pos):
    cos, sin = _rope_angles(pos)
    rows = pos.shape[0]
    ones = jnp.ones((rows, HEAD_DIM - ROT_DIM), F32)
    zeros = jnp.zeros((rows, HEAD_DIM - ROT_DIM), F32)
    zh = jnp.zeros((rows, ROT_HALF), F32)
    cos64 = jnp.concatenate([cos, cos, ones], axis=1)
    sa64 = jnp.concatenate([-sin, zh, zeros], axis=1)
    sb64 = jnp.concatenate([zh, sin, zeros], axis=1)
    tile2 = lambda t: jnp.concatenate([t, t], axis=1)
    return tile2(cos64), tile2(sa64), tile2(sb64)


def _conv_prompt_kernel(x_ref, w_in_ref, w_conv_ref, w_out_ref, g_ref, b_ref,
                        y_ref, state_ref, ubuf_ref):
    t = x_ref.shape[1]
    pad = SUBLANES

    @pl.when(pl.program_id(1) == 0)
    def _():
        ubuf_ref[0:pad, :] = jnp.zeros((pad, D_MODEL), F32)

    x = x_ref[0]
    xb = x.astype(BF16)
    c = _dot(xb, w_in_ref[:, D_MODEL:2 * D_MODEL])
    v = _dot(xb, w_in_ref[:, 2 * D_MODEL:3 * D_MODEL])
    u = c * v
    ubuf_ref[pad:pad + t, :] = u
    conv = (w_conv_ref[0:1, :] * ubuf_ref[pad - 2:pad - 2 + t, :]
            + w_conv_ref[1:2, :] * ubuf_ref[pad - 1:pad - 1 + t, :]
            + w_conv_ref[2:3, :] * u)
    last2 = u[t - 2:t, :]
    ubuf_ref[pad - 2:pad, :] = last2
    state_ref[0] = last2
    bgate = _dot(xb, w_in_ref[:, 0:D_MODEL])
    z = _dot(xb, w_in_ref[:, 3 * D_MODEL:4 * D_MODEL])
    h = _dot((_silu(z) * bgate * conv).astype(BF16), w_out_ref[...])
    y_ref[0] = _layer_norm(ALPHA * x + h, g_ref[...], b_ref[...])


def _conv_prompt(x, w_in, w_conv, w_out, g, b):
    bsz, s, _ = x.shape
    t = CONV_TILE
    const = lambda *_: (0, 0)
    return pl.pallas_call(
        _conv_prompt_kernel,
        grid=(bsz, s // t),
        in_specs=[
            pl.BlockSpec((1, t, D_MODEL), lambda i, j: (i, j, 0)),
            pl.BlockSpec((D_MODEL, 4 * D_MODEL), const, pipeline_mode=pl.Buffered(1)),
            pl.BlockSpec((CONV_WIDTH, D_MODEL), const),
            pl.BlockSpec((D_MODEL, D_MODEL), const, pipeline_mode=pl.Buffered(1)),
            pl.BlockSpec((1, D_MODEL), const),
            pl.BlockSpec((1, D_MODEL), const),
        ],
        out_specs=[
            pl.BlockSpec((1, t, D_MODEL), lambda i, j: (i, j, 0)),
            pl.BlockSpec((1, CONV_WIDTH - 1, D_MODEL), lambda i, j: (i, 0, 0)),
        ],
        out_shape=[
            jax.ShapeDtypeStruct((bsz, s, D_MODEL), F32),
            jax.ShapeDtypeStruct((bsz, CONV_WIDTH - 1, D_MODEL), F32),
        ],
        scratch_shapes=[pltpu.VMEM((t + SUBLANES, D_MODEL), F32)],
        compiler_params=pltpu.CompilerParams(
            dimension_semantics=("arbitrary", "arbitrary"),
            vmem_limit_bytes=VMEM_LIMIT_BYTES),
        name="conv_prompt",
    )(x, w_in, w_conv, w_out, g, b)


def _attn_prompt_kernel(x_ref, w_qvz_ref, w_kt_ref, w_out_ref, cos_ref, sa_ref, sb_ref,
                        cos_t_ref, sin_t_ref, lamv_ref, subln_ref, g_ref, b_ref,
                        kt_out_ref, v_out_ref, y_ref,
                        kc_ref, vc_ref, qs_ref, of_ref, m_ref, acc_ref):
    t = x_ref.shape[1]
    sub = ATTN_SUB
    n_sub = t // sub
    qi = pl.program_id(1)

    x = x_ref[0]
    xb = x.astype(BF16)

    kt = _dot_nt(w_kt_ref[...], xb)
    cos_c, sin_c = cos_t_ref[...], sin_t_ref[...]
    for h in range(N_HEADS):
        pieces = []
        for c in range(2):
            r0 = h * LANES + c * HEAD_DIM
            x1 = kt[r0:r0 + ROT_HALF, :]
            x2 = kt[r0 + ROT_HALF:r0 + ROT_DIM, :]
            pieces += [x1 * cos_c - x2 * sin_c, x2 * cos_c + x1 * sin_c,
                       kt[r0 + ROT_DIM:r0 + HEAD_DIM, :]]
        kth = jnp.concatenate(pieces, axis=0)
        kt_out_ref[0, h * LANES:(h + 1) * LANES, :] = kth
        for a in range(n_sub):
            kc_ref[h, qi * n_sub + a] = kth[:, a * sub:(a + 1) * sub].astype(BF16)

    v = _dot(xb, w_qvz_ref[:, D_MODEL:2 * D_MODEL])
    for h in range(N_HEADS):
        vh = v[:, h * LANES:(h + 1) * LANES]
        v_out_ref[0, pl.ds(h, t, stride=N_HEADS), :] = vh
        for a in range(n_sub):
            vc_ref[h, qi * n_sub + a] = vh[a * sub:(a + 1) * sub, :].astype(BF16)

    q = _dot(xb, w_qvz_ref[:, 0:D_MODEL])
    cos_r, sin_a, sin_b = cos_ref[...], sa_ref[...], sb_ref[...]
    first_map = lax.broadcasted_iota(jnp.int32, (t, LANES), 1) < HEAD_DIM
    for h in range(N_HEADS):
        qh = _rope_head(q[:, h * LANES:(h + 1) * LANES], cos_r, sin_a, sin_b) * (HEAD_DIM ** -0.5 * LOG2E)
        q1 = jnp.where(first_map, qh, 0.0).astype(BF16)
        q2 = jnp.where(first_map, 0.0, qh).astype(BF16)
        for a in range(n_sub):
            rs = slice(a * sub, (a + 1) * sub)
            qs_ref[h * n_sub + a, 0:sub, :] = q1[rs]
            qs_ref[h * n_sub + a, sub:2 * sub, :] = q2[rs]

    lam = _lambda_full(lamv_ref[...])
    subln_g = subln_ref[...]
    rows2 = lax.broadcasted_iota(jnp.int32, (2 * sub, sub), 0)
    cols2 = lax.broadcasted_iota(jnp.int32, (2 * sub, sub), 1)
    causal = cols2 <= jnp.where(rows2 >= sub, rows2 - sub, rows2)

    ones_cols = jnp.ones((sub, LANES), BF16)

    def block_update(a, h, kb, mask):
        s = _dot(qs_ref[h * n_sub + a], kc_ref[h, kb])
        if mask is not None:
            s = jnp.where(mask, s, NEG_INF)
        m_prev = m_ref[h]
        m_new = jnp.maximum(m_prev, jnp.max(s, axis=-1, keepdims=True))
        corr = jnp.exp2(m_prev - m_new)
        p = jnp.exp2(s - jnp.concatenate([m_new] * (sub // LANES), axis=1)).astype(BF16)
        v_ext = jnp.concatenate([vc_ref[h, kb], ones_cols], axis=1)
        acc_ref[h] = acc_ref[h] * jnp.concatenate([corr, corr], axis=1) + _dot(p, v_ext)
        m_ref[h] = m_new

    for a in range(n_sub):
        m_ref[...] = jnp.full(m_ref.shape, NEG_INF, F32)
        acc_ref[...] = jnp.zeros(acc_ref.shape, F32)
        n_full = qi * n_sub + a

        def kv_body(kb, c, a=a):
            for h in range(N_HEADS):
                block_update(a, h, kb, None)
            return c

        lax.fori_loop(0, n_full, kv_body, 0)
        for h in range(N_HEADS):
            block_update(a, h, n_full, causal)
        for h in range(N_HEADS):
            acc = acc_ref[h]
            o = acc[:, 0:LANES] / acc[:, LANES:2 * LANES]
            diff = o[0:sub, :] - lam * o[sub:2 * sub, :]
            of_ref[a * sub:(a + 1) * sub, h * LANES:(h + 1) * LANES] = _subln_gate(diff, subln_g)

    z = _dot(xb, w_qvz_ref[:, 2 * D_MODEL:3 * D_MODEL])
    hout = _dot((of_ref[...] * _silu(z)).astype(BF16), w_out_ref[...])
    y_ref[0] = _layer_norm(ALPHA * x + hout, g_ref[...], b_ref[...])


def _attn_prompt(x, w_qvz, w_kt, w_out, tables, tables_t, lamv, subln_g, g, b):
    bsz, s, _ = x.shape
    t = ATTN_TILE
    sub = ATTN_SUB
    const = lambda *_: (0, 0)
    weight = lambda shape: pl.BlockSpec(shape, const, pipeline_mode=pl.Buffered(1))
    tile = pl.BlockSpec((1, t, D_MODEL), lambda i, j: (i, j, 0))
    table = pl.BlockSpec((t, LANES), lambda i, j: (j, 0))
    table_t = pl.BlockSpec((ROT_HALF, t), lambda i, j: (0, j))
    return pl.pallas_call(
        _attn_prompt_kernel,
        grid=(bsz, s // t),
        in_specs=[
            tile,
            weight((D_MODEL, 3 * D_MODEL)), weight((D_MODEL, D_MODEL)), weight((D_MODEL, D_MODEL)),
            table, table, table, table_t, table_t,
            pl.BlockSpec((4, HEAD_DIM), const),
            pl.BlockSpec((1, V_DIM), const),
            pl.BlockSpec((1, D_MODEL), const),
            pl.BlockSpec((1, D_MODEL), const),
        ],
        out_specs=[
            pl.BlockSpec((1, D_MODEL, t), lambda i, j: (i, 0, j)),
            pl.BlockSpec((1, t * N_HEADS, V_DIM), lambda i, j: (i, j, 0)),
            tile,
        ],
        out_shape=[
            jax.ShapeDtypeStruct((bsz, D_MODEL, s), F32),
            jax.ShapeDtypeStruct((bsz, s * N_HEADS, V_DIM), F32),
            jax.ShapeDtypeStruct((bsz, s, D_MODEL), F32),
        ],
        scratch_shapes=[
            pltpu.VMEM((N_HEADS, s // sub, LANES, sub), BF16),
            pltpu.VMEM((N_HEADS, s // sub, sub, LANES), BF16),
            pltpu.VMEM((N_HEADS * (t // sub), 2 * sub, LANES), BF16),
            pltpu.VMEM((t, D_MODEL), F32),
            pltpu.VMEM((N_HEADS, 2 * sub, LANES), F32),
            pltpu.VMEM((N_HEADS, 2 * sub, 2 * LANES), F32),
        ],
        compiler_params=pltpu.CompilerParams(
            dimension_semantics=("arbitrary", "arbitrary"),
            vmem_limit_bytes=VMEM_LIMIT_BYTES),
        name="attn_prompt",
    )(x, w_qvz, w_kt, w_out, *tables, *tables_t, lamv, subln_g, g, b)


def _sample_pre_kernel(x_ref, past_ref, cw_in_ref, w_conv_ref, cw_out_ref, cg_ref, cb_ref,
                       aw_in_ref, cos_ref, sa_ref, sb_ref,
                       state_ref, y1_ref, q_ref, k_ref, v_ref, z_ref):
    x = x_ref[...]
    xb = x.astype(BF16)
    past0 = past_ref[:, 0:D_MODEL]
    past1 = past_ref[:, D_MODEL:2 * D_MODEL]
    bgate = _dot(xb, cw_in_ref[:, 0:D_MODEL])
    c = _dot(xb, cw_in_ref[:, D_MODEL:2 * D_MODEL])
    v = _dot(xb, cw_in_ref[:, 2 * D_MODEL:3 * D_MODEL])
    z = _dot(xb, cw_in_ref[:, 3 * D_MODEL:4 * D_MODEL])
    u = c * v
    conv = w_conv_ref[0:1, :] * past0 + w_conv_ref[1:2, :] * past1 + w_conv_ref[2:3, :] * u
    state_ref[:, 0:D_MODEL] = past1
    state_ref[:, D_MODEL:2 * D_MODEL] = u
    h = _dot((_silu(z) * bgate * conv).astype(BF16), cw_out_ref[...])
    y1 = _layer_norm(ALPHA * x + h, cg_ref[...], cb_ref[...])
    y1_ref[...] = y1

    yb = y1.astype(BF16)
    cos_t, sin_a, sin_b = cos_ref[...], sa_ref[...], sb_ref[...]
    q = _dot(yb, aw_in_ref[:, 0:D_MODEL])
    k = _dot(yb, aw_in_ref[:, D_MODEL:2 * D_MODEL])
    v_ref[...] = _dot(yb, aw_in_ref[:, 2 * D_MODEL:3 * D_MODEL])
    z_ref[...] = _dot(yb, aw_in_ref[:, 3 * D_MODEL:4 * D_MODEL])
    for h_i in range(N_HEADS):
        hs = slice(h_i * LANES, (h_i + 1) * LANES)
        k_ref[:, hs] = _rope_head(k[:, hs], cos_t, sin_a, sin_b)
        q_ref[:, hs] = _rope_head(q[:, hs], cos_t, sin_a, sin_b) * (HEAD_DIM ** -0.5)


def _sample_pre(x, past, cw_in, w_conv, cw_out, cg, cb, aw_in, tables):
    n = x.shape[0]
    full = lambda shape: pl.BlockSpec(shape, lambda i: (0,) * len(shape))
    weight = lambda shape: pl.BlockSpec(shape, lambda i: (0, 0), pipeline_mode=pl.Buffered(1))
    row = jax.ShapeDtypeStruct((n, D_MODEL), F32)
    return pl.pallas_call(
        _sample_pre_kernel,
        grid=(1,),
        in_specs=[
            full((n, D_MODEL)), full((n, 2 * D_MODEL)),
            weight((D_MODEL, 4 * D_MODEL)), full((CONV_WIDTH, D_MODEL)), weight((D_MODEL, D_MODEL)),
            full((1, D_MODEL)), full((1, D_MODEL)),
            weight((D_MODEL, 4 * D_MODEL)),
            full((1, LANES)), full((1, LANES)), full((1, LANES)),
        ],
        out_specs=[full((n, 2 * D_MODEL))] + [full((n, D_MODEL))] * 5,
        out_shape=[jax.ShapeDtypeStruct((n, 2 * D_MODEL), F32)] + [row] * 5,
        compiler_params=pltpu.CompilerParams(
            dimension_semantics=("arbitrary",), vmem_limit_bytes=VMEM_LIMIT_BYTES),
        name="sample_pre",
    )(x, past, cw_in, w_conv, cw_out, cg, cb, aw_in, *tables)


def _sample_decode_kernel(pt_ref, q_ref, knew_ref, vnew_ref, lamv_ref, *refs):
    del pt_ref
    n = PAGES_PER_STEP
    kt_refs, v_refs = refs[0:n], refs[n:2 * n]
    o_ref = refs[2 * n]
    qm_ref, m_ref, l_ref, acc_ref = refs[2 * n + 1:]
    j = pl.program_id(1)
    n_maps = 2 * N_HEADS
    rows = lax.broadcasted_iota(jnp.int32, (n_maps, D_MODEL), 0)
    lanes = lax.broadcasted_iota(jnp.int32, (n_maps, D_MODEL), 1)

    @pl.when(j == 0)
    def _():
        qm_ref[...] = jnp.where(lanes // HEAD_DIM == rows, q_ref[...], 0.0).astype(BF16)
        m_ref[...] = jnp.full(m_ref.shape, NEG_INF, F32)
        l_ref[...] = jnp.zeros(l_ref.shape, F32)
        acc_ref[...] = jnp.zeros(acc_ref.shape, F32)

    qm = qm_ref[...]
    s = jnp.concatenate([_dot(qm, kt_refs[i][...].astype(BF16)) for i in range(n)], axis=1)
    m_prev = m_ref[...]
    m_new = jnp.maximum(m_prev, jnp.max(s, axis=-1, keepdims=True))
    corr = jnp.exp(m_prev - m_new)
    p = jnp.exp(s - m_new)
    l_new = l_ref[...] * corr + jnp.sum(p, axis=-1, keepdims=True)
    pb = p.astype(BF16)

    def page_values(i):
        return jnp.concatenate(
            [v_refs[i][pl.ds(h, PAGE_SIZE, stride=N_HEADS), :] for h in range(N_HEADS)],
            axis=1).astype(BF16)

    pv = _dot(pb[:, 0:PAGE_SIZE], page_values(0))
    for i in range(1, n):
        pv += _dot(pb[:, i * PAGE_SIZE:(i + 1) * PAGE_SIZE], page_values(i))
    acc_new = acc_ref[...] * corr + pv
    m_ref[...] = m_new
    l_ref[...] = l_new
    acc_ref[...] = acc_new

    @pl.when(j == pl.num_programs(1) - 1)
    def _():
        knew = knew_ref[...].astype(BF16).astype(F32)
        vnew = vnew_ref[...].astype(BF16).astype(F32)
        s_self = jnp.sum(qm.astype(F32) * knew, axis=-1, keepdims=True)
        m_fin = jnp.maximum(m_new, s_self)
        corr_f = jnp.exp(m_new - m_fin)
        p_self = jnp.exp(s_self - m_fin)
        l_fin = l_new * corr_f + p_self
        acc_fin = acc_new * corr_f + p_self.astype(BF16).astype(F32) * vnew
        o = acc_fin / l_fin
        lam = _lambda_full(lamv_ref[...])
        own_head = lanes // V_DIM == rows // 2
        weight = jnp.where(rows % 2 == 0, 1.0, -lam)
        o_ref[...] = jnp.sum(jnp.where(own_head, o * weight, 0.0), axis=0, keepdims=True)


def _sample_decode(page_table, q, k_new, v_new, lamv, cache_kt, cache_v):
    n = q.shape[0]
    n_pages = page_table.shape[1]
    pps = PAGES_PER_STEP
    pt = page_table.reshape(-1)
    row = pl.BlockSpec((None, 1, D_MODEL), lambda b, j, pt_ref: (b, 0, 0))

    def page_spec(i, shape):
        return pl.BlockSpec((None,) + shape,
                            lambda b, j, pt_ref: (pt_ref[b * n_pages + j * pps + i], 0, 0))

    grid_spec = pltpu.PrefetchScalarGridSpec(
        num_scalar_prefetch=1,
        grid=(n, n_pages // pps),
        in_specs=[row, row, row, pl.BlockSpec((4, HEAD_DIM), lambda b, j, pt_ref: (0, 0))]
        + [page_spec(i, (D_MODEL, PAGE_SIZE)) for i in range(pps)]
        + [page_spec(i, (PAGE_SIZE * N_HEADS, V_DIM)) for i in range(pps)],
        out_specs=row,
        scratch_shapes=[
            pltpu.VMEM((2 * N_HEADS, D_MODEL), BF16),
            pltpu.VMEM((2 * N_HEADS, 1), F32),
            pltpu.VMEM((2 * N_HEADS, 1), F32),
            pltpu.VMEM((2 * N_HEADS, D_MODEL), F32),
        ],
    )
    r3 = lambda a: a.reshape(n, 1, D_MODEL)
    out = pl.pallas_call(
        _sample_decode_kernel,
        grid_spec=grid_spec,
        out_shape=jax.ShapeDtypeStruct((n, 1, D_MODEL), F32),
        compiler_params=pltpu.CompilerParams(
            dimension_semantics=("arbitrary", "arbitrary"),
            vmem_limit_bytes=VMEM_LIMIT_BYTES),
        name="sample_decode",
    )(pt, r3(q), r3(k_new), r3(v_new), lamv, *([cache_kt] * pps), *([cache_v] * pps))
    return out.reshape(n, D_MODEL)


def _sample_post_kernel(diff_ref, z_ref, y1_ref, subln_ref, w_out_ref, g_ref, b_ref, y_ref):
    subln_g = subln_ref[...]
    of = jnp.concatenate(
        [_subln_gate(diff_ref[:, h * LANES:(h + 1) * LANES], subln_g) for h in range(N_HEADS)], axis=1)
    hout = _dot((of * _silu(z_ref[...])).astype(BF16), w_out_ref[...])
    y_ref[...] = _layer_norm(ALPHA * y1_ref[...] + hout, g_ref[...], b_ref[...])


def _sample_post(diff, z, y1, subln_g, w_out, g, b):
    n = diff.shape[0]
    full = lambda shape: pl.BlockSpec(shape, lambda i: (0,) * len(shape))
    return pl.pallas_call(
        _sample_post_kernel,
        grid=(1,),
        in_specs=[full((n, D_MODEL))] * 3
        + [full((1, V_DIM)), full((D_MODEL, D_MODEL)), full((1, D_MODEL)), full((1, D_MODEL))],
        out_specs=full((n, D_MODEL)),
        out_shape=jax.ShapeDtypeStruct((n, D_MODEL), F32),
        compiler_params=pltpu.CompilerParams(dimension_semantics=("arbitrary",)),
        name="sample_post",
    )(diff, z, y1, subln_g, w_out, g, b)


def kernel(x_prompt, x_sample, state_conv, cache_k, cache_v, page_table, conv_w_in, conv_w, conv_w_out,
           ln_conv_g, ln_conv_b, attn_w_in, lambda_q1, lambda_k1, lambda_q2, lambda_k2, subln_g,
           attn_w_out, ln_attn_g, ln_attn_b):
    bsz, seq, _ = x_prompt.shape
    n_dec = x_sample.shape[0]
    pool = cache_k.shape[0] * cache_k.shape[1]

    cw_in = conv_w_in[0].astype(BF16)
    cw_out = conv_w_out[0].astype(BF16)
    aw_in = attn_w_in[0].astype(BF16)
    aw_qvz = jnp.concatenate([aw_in[:, 0:D_MODEL], aw_in[:, 2 * D_MODEL:4 * D_MODEL]], axis=1)
    aw_kt = aw_in[:, D_MODEL:2 * D_MODEL].T
    aw_out = attn_w_out[0].astype(BF16)
    w_conv = conv_w[0]
    cg, cb = ln_conv_g[0:1], ln_conv_b[0:1]
    ag, ab = ln_attn_g[0:1], ln_attn_b[0:1]
    lamv = jnp.concatenate(
        [lambda_q1[0:1], lambda_k1[0:1], lambda_q2[0:1], lambda_k2[0:1]], axis=0)
    sg = subln_g[0:1]
    pos_p = jnp.arange(seq)
    tables_p = _rope_tables(pos_p)
    tables_pt = tuple(a.T for a in _rope_angles(pos_p))
    tables_s = _rope_tables(PAST_LEN + jnp.arange(1))

    y1_p, conv_state_p = _conv_prompt(x_prompt, cw_in, w_conv, cw_out, cg, cb)
    kt_p, v_p, y_p = _attn_prompt(y1_p, aw_qvz, aw_kt, aw_out, tables_p, tables_pt, lamv, sg, ag, ab)
    k_p = kt_p.reshape(1, bsz, N_HEADS, 2, HEAD_DIM, seq).transpose(0, 1, 5, 2, 3, 4)

    past = state_conv[0].reshape(n_dec, (CONV_WIDTH - 1) * D_MODEL)
    state_s, y1_s, q_s, k_s, v_s, z_s = _sample_pre(
        x_sample.reshape(n_dec, D_MODEL), past, cw_in, w_conv, cw_out, cg, cb, aw_in, tables_s)
    cache_kt = cache_k.transpose(0, 1, 3, 4, 5, 2).reshape(pool, D_MODEL, PAGE_SIZE)
    cache_vr = cache_v.reshape(pool, PAGE_SIZE * N_HEADS, V_DIM)
    diff_s = _sample_decode(page_table, q_s, k_s, v_s, lamv, cache_kt, cache_vr)
    y_s = _sample_post(diff_s, z_s, y1_s, sg, aw_out, ag, ab)

    return (
        y_p,
        y_s.reshape(n_dec, 1, D_MODEL),
        conv_state_p[None],
        state_s.reshape(1, n_dec, CONV_WIDTH - 1, D_MODEL),
        k_p,
        v_p.reshape(1, bsz, seq, N_HEADS, V_DIM),
        k_s.reshape(1, n_dec, 1, N_HEADS, 2, HEAD_DIM),
        v_s.reshape(1, n_dec, 1, N_HEADS, V_DIM),
    )
```

```python
import functools
import math

import jax
import jax.numpy as jnp
from jax import lax
from jax.experimental import pallas as pl
from jax.experimental.pallas import tpu as pltpu

D_MODEL = 1024
DEPTH = 2
PAST_LEN = 16384
PAGE_SIZE = 128
CONV_WIDTH = 3
N_HEADS = 8
HEAD_DIM = D_MODEL // (2 * N_HEADS)
V_DIM = 2 * HEAD_DIM
ROT_DIM = HEAD_DIM // 4
ROT_HALF = ROT_DIM // 2
ROPE_THETA = 500000.0
LN_EPS = 1e-5
RMS_EPS = 1e-5
ALPHA = (2 * DEPTH) ** 0.25
NEG_INF = -1e30
LAM_INIT = 0.8 - 0.6 * math.exp(-0.3 * 1)
LOG2E = math.log2(math.e)

LANES = 128
SUBLANES = 8
VMEM_LIMIT_BYTES = 56 * 1024 * 1024

CONV_TILE = 256
ATTN_TILE = 512
ATTN_SUB = 256
PAGES_PER_STEP = 8
DEC_RING_STEPS = 3

F32 = jnp.float32
BF16 = jnp.bfloat16

assert ROT_HALF == SUBLANES


def _dot(a, b):
    return jnp.dot(a, b, preferred_element_type=F32)


def _dot_nt(a, b):
    return lax.dot_general(a, b, (((1,), (1,)), ((), ())), preferred_element_type=F32)


def _layer_norm(x, g, b):
    mu = jnp.mean(x, axis=-1, keepdims=True)
    xc = x - mu
    var = jnp.mean(xc * xc, axis=-1, keepdims=True)
    return xc * lax.rsqrt(var + LN_EPS) * g + b


def _silu(z):
    return z * (1.0 / (1.0 + jnp.exp(-z)))


def _rope_head(x, cos_t, sin_a, sin_b):
    up = pltpu.roll(x, LANES - ROT_HALF, axis=1)
    dn = pltpu.roll(x, ROT_HALF, axis=1)
    return x * cos_t + up * sin_a + dn * sin_b


def _lambda_full(lamv):
    a = jnp.sum(lamv[0:1, :] * lamv[1:2, :], axis=-1, keepdims=True)
    b = jnp.sum(lamv[2:3, :] * lamv[3:4, :], axis=-1, keepdims=True)
    return jnp.exp(a) - jnp.exp(b) + LAM_INIT


def _subln_gate(diff, subln_g):
    ms = jnp.mean(diff * diff, axis=-1, keepdims=True)
    return diff * lax.rsqrt(ms + RMS_EPS) * subln_g * (1.0 - LAM_INIT)


def _rope_angles(pos):
    inv_freq = jnp.power(ROPE_THETA, -jnp.arange(ROT_HALF, dtype=F32) * (2.0 / ROT_DIM))
    ang = pos.astype(F32)[:, None] * inv_freq[None, :]
    return jnp.cos(ang), jnp.sin(ang)


def _rope_tables(pos):
    cos, sin = _rope_angles(pos)
    rows = pos.shape[0]
    ones = jnp.ones((rows, HEAD_DIM - ROT_DIM), F32)
    zeros = jnp.zeros((rows, HEAD_DIM - ROT_DIM), F32)
    zh = jnp.zeros((rows, ROT_HALF), F32)
    cos64 = jnp.concatenate([cos, cos, ones], axis=1)
    sa64 = jnp.concatenate([-sin, zh, zeros], axis=1)
    sb64 = jnp.concatenate([zh, sin, zeros], axis=1)
    tile2 = lambda t: jnp.concatenate([t, t], axis=1)
    return tile2(cos64), tile2(sa64), tile2(sb64)


def _conv_prompt_tile(first_tile, x_ref, w_in_ref, w_conv_ref, w_out_ref, g_ref, b_ref,
                      y_ref, state_ref, ubuf_ref):
    t = x_ref.shape[1]
    pad = SUBLANES

    @pl.when(first_tile)
    def _():
        ubuf_ref[0:pad, :] = jnp.zeros((pad, D_MODEL), F32)

    x = x_ref[0]
    xb = x.astype(BF16)
    c = _dot(xb, w_in_ref[:, D_MODEL:2 * D_MODEL])
    v = _dot(xb, w_in_ref[:, 2 * D_MODEL:3 * D_MODEL])
    u = c * v
    ubuf_ref[pad:pad + t, :] = u
    conv = (w_conv_ref[0:1, :] * ubuf_ref[pad - 2:pad - 2 + t, :]
            + w_conv_ref[1:2, :] * ubuf_ref[pad - 1:pad - 1 + t, :]
            + w_conv_ref[2:3, :] * u)
    last2 = u[t - 2:t, :]
    ubuf_ref[pad - 2:pad, :] = last2
    state_ref[0] = last2
    bgate = _dot(xb, w_in_ref[:, 0:D_MODEL])
    z = _dot(xb, w_in_ref[:, 3 * D_MODEL:4 * D_MODEL])
    h = _dot((_silu(z) * bgate * conv).astype(BF16), w_out_ref[...])
    y_ref[0] = _layer_norm(ALPHA * x + h, g_ref[...], b_ref[...])


def _attn_prompt_kernel(x_ref, w_qvz_ref, w_kt_ref, w_out_ref, cos_ref, sa_ref, sb_ref,
                        cos_t_ref, sin_t_ref, lamv_ref, subln_ref, g_ref, b_ref,
                        kt_out_ref, v_out_ref, y_ref,
                        kc_ref, vc_ref, qs_ref, of_ref, m_ref, acc_ref):
    t = x_ref.shape[1]
    sub = ATTN_SUB
    n_sub = t // sub
    qi = pl.program_id(1)

    x = x_ref[0]
    xb = x.astype(BF16)

    kt = _dot_nt(w_kt_ref[...], xb)
    cos_c, sin_c = cos_t_ref[...], sin_t_ref[...]
    for h in range(N_HEADS):
        pieces = []
        for c in range(2):
            r0 = h * LANES + c * HEAD_DIM
            x1 = kt[r0:r0 + ROT_HALF, :]
            x2 = kt[r0 + ROT_HALF:r0 + ROT_DIM, :]
            pieces += [x1 * cos_c - x2 * sin_c, x2 * cos_c + x1 * sin_c,
                       kt[r0 + ROT_DIM:r0 + HEAD_DIM, :]]
        kth = jnp.concatenate(pieces, axis=0)
        kt_out_ref[0, h * LANES:(h + 1) * LANES, :] = kth
        for a in range(n_sub):
            kc_ref[h, qi * n_sub + a] = kth[:, a * sub:(a + 1) * sub].astype(BF16)

    v = _dot(xb, w_qvz_ref[:, D_MODEL:2 * D_MODEL])
    for h in range(N_HEADS):
        vh = v[:, h * LANES:(h + 1) * LANES]
        v_out_ref[0, pl.ds(h, t, stride=N_HEADS), :] = vh
        for a in range(n_sub):
            vc_ref[h, qi * n_sub + a] = vh[a * sub:(a + 1) * sub, :].astype(BF16)

    q = _dot(xb, w_qvz_ref[:, 0:D_MODEL])
    cos_r, sin_a, sin_b = cos_ref[...], sa_ref[...], sb_ref[...]
    first_map = lax.broadcasted_iota(jnp.int32, (t, LANES), 1) < HEAD_DIM
    for h in range(N_HEADS):
        qh = _rope_head(q[:, h * LANES:(h + 1) * LANES], cos_r, sin_a, sin_b) * (HEAD_DIM ** -0.5 * LOG2E)
        q1 = jnp.where(first_map, qh, 0.0).astype(BF16)
        q2 = jnp.where(first_map, 0.0, qh).astype(BF16)
        for a in range(n_sub):
            rs = slice(a * sub, (a + 1) * sub)
            qs_ref[h * n_sub + a, 0:sub, :] = q1[rs]
            qs_ref[h * n_sub + a, sub:2 * sub, :] = q2[rs]

    lam = _lambda_full(lamv_ref[...])
    subln_g = subln_ref[...]
    rows2 = lax.broadcasted_iota(jnp.int32, (2 * sub, sub), 0)
    cols2 = lax.broadcasted_iota(jnp.int32, (2 * sub, sub), 1)
    causal = cols2 <= jnp.where(rows2 >= sub, rows2 - sub, rows2)

    ones_cols = jnp.ones((sub, LANES), BF16)

    def block_update(a, h, kb, mask):
        s = _dot(qs_ref[h * n_sub + a], kc_ref[h, kb])
        if mask is not None:
            s = jnp.where(mask, s, NEG_INF)
        m_prev = m_ref[h]
        m_new = jnp.maximum(m_prev, jnp.max(s, axis=-1, keepdims=True))
        corr = jnp.exp2(m_prev - m_new)
        p = jnp.exp2(s - jnp.concatenate([m_new] * (sub // LANES), axis=1)).astype(BF16)
        v_ext = jnp.concatenate([vc_ref[h, kb], ones_cols], axis=1)
        acc_ref[h] = acc_ref[h] * jnp.concatenate([corr, corr], axis=1) + _dot(p, v_ext)
        m_ref[h] = m_new

    for a in range(n_sub):
        m_ref[...] = jnp.full(m_ref.shape, NEG_INF, F32)
        acc_ref[...] = jnp.zeros(acc_ref.shape, F32)
        n_full = qi * n_sub + a

        def kv_body(kb, c, a=a):
            for h in range(N_HEADS):
                block_update(a, h, kb, None)
            return c

        lax.fori_loop(0, n_full, kv_body, 0)
        for h in range(N_HEADS):
            block_update(a, h, n_full, causal)
        for h in range(N_HEADS):
            acc = acc_ref[h]
            o = acc[:, 0:LANES] / acc[:, LANES:2 * LANES]
            diff = o[0:sub, :] - lam * o[sub:2 * sub, :]
            of_ref[a * sub:(a + 1) * sub, h * LANES:(h + 1) * LANES] = _subln_gate(diff, subln_g)

    z = _dot(xb, w_qvz_ref[:, 2 * D_MODEL:3 * D_MODEL])
    hout = _dot((of_ref[...] * _silu(z)).astype(BF16), w_out_ref[...])
    y_ref[0] = _layer_norm(ALPHA * x + hout, g_ref[...], b_ref[...])


def _attn_prompt(x, w_qvz, w_kt, w_out, tables, tables_t, lamv, subln_g, g, b):
    bsz, s, _ = x.shape
    t = ATTN_TILE
    sub = ATTN_SUB
    const = lambda *_: (0, 0)
    weight = lambda shape: pl.BlockSpec(shape, const, pipeline_mode=pl.Buffered(1))
    tile = pl.BlockSpec((1, t, D_MODEL), lambda i, j: (i, j, 0))
    table = pl.BlockSpec((t, LANES), lambda i, j: (j, 0))
    table_t = pl.BlockSpec((ROT_HALF, t), lambda i, j: (0, j))
    return pl.pallas_call(
        _attn_prompt_kernel,
        grid=(bsz, s // t),
        in_specs=[
            tile,
            weight((D_MODEL, 3 * D_MODEL)), weight((D_MODEL, D_MODEL)), weight((D_MODEL, D_MODEL)),
            table, table, table, table_t, table_t,
            pl.BlockSpec((4, HEAD_DIM), const),
            pl.BlockSpec((1, V_DIM), const),
            pl.BlockSpec((1, D_MODEL), const),
            pl.BlockSpec((1, D_MODEL), const),
        ],
        out_specs=[
            pl.BlockSpec((1, D_MODEL, t), lambda i, j: (i, 0, j)),
            pl.BlockSpec((1, t * N_HEADS, V_DIM), lambda i, j: (i, j, 0)),
            tile,
        ],
        out_shape=[
            jax.ShapeDtypeStruct((bsz, D_MODEL, s), F32),
            jax.ShapeDtypeStruct((bsz, s * N_HEADS, V_DIM), F32),
            jax.ShapeDtypeStruct((bsz, s, D_MODEL), F32),
        ],
        scratch_shapes=[
            pltpu.VMEM((N_HEADS, s // sub, LANES, sub), BF16),
            pltpu.VMEM((N_HEADS, s // sub, sub, LANES), BF16),
            pltpu.VMEM((N_HEADS * (t // sub), 2 * sub, LANES), BF16),
            pltpu.VMEM((t, D_MODEL), F32),
            pltpu.VMEM((N_HEADS, 2 * sub, LANES), F32),
            pltpu.VMEM((N_HEADS, 2 * sub, 2 * LANES), F32),
        ],
        compiler_params=pltpu.CompilerParams(
            dimension_semantics=("arbitrary", "arbitrary"),
            vmem_limit_bytes=VMEM_LIMIT_BYTES),
        name="attn_prompt",
    )(x, w_qvz, w_kt, w_out, *tables, *tables_t, lamv, subln_g, g, b)


def _sample_pre_kernel(x_ref, past_ref, cw_in_ref, w_conv_ref, cw_out_ref, cg_ref, cb_ref,
                       aw_in_ref, cos_ref, sa_ref, sb_ref,
                       state_ref, y1_ref, q_ref, k_ref, v_ref, z_ref):
    x = x_ref[...]
    xb = x.astype(BF16)
    past0 = past_ref[:, 0:D_MODEL]
    past1 = past_ref[:, D_MODEL:2 * D_MODEL]
    bgate = _dot(xb, cw_in_ref[:, 0:D_MODEL])
    c = _dot(xb, cw_in_ref[:, D_MODEL:2 * D_MODEL])
    v = _dot(xb, cw_in_ref[:, 2 * D_MODEL:3 * D_MODEL])
    z = _dot(xb, cw_in_ref[:, 3 * D_MODEL:4 * D_MODEL])
    u = c * v
    conv = w_conv_ref[0:1, :] * past0 + w_conv_ref[1:2, :] * past1 + w_conv_ref[2:3, :] * u
    state_ref[:, 0:D_MODEL] = past1
    state_ref[:, D_MODEL:2 * D_MODEL] = u
    h = _dot((_silu(z) * bgate * conv).astype(BF16), cw_out_ref[...])
    y1 = _layer_norm(ALPHA * x + h, cg_ref[...], cb_ref[...])
    y1_ref[...] = y1

    yb = y1.astype(BF16)
    cos_t, sin_a, sin_b = cos_ref[...], sa_ref[...], sb_ref[...]
    q = _dot(yb, aw_in_ref[:, 0:D_MODEL])
    k = _dot(yb, aw_in_ref[:, D_MODEL:2 * D_MODEL])
    v_ref[...] = _dot(yb, aw_in_ref[:, 2 * D_MODEL:3 * D_MODEL])
    z_ref[...] = _dot(yb, aw_in_ref[:, 3 * D_MODEL:4 * D_MODEL])
    for h_i in range(N_HEADS):
        hs = slice(h_i * LANES, (h_i + 1) * LANES)
        k_ref[:, hs] = _rope_head(k[:, hs], cos_t, sin_a, sin_b)
        q_ref[:, hs] = _rope_head(q[:, hs], cos_t, sin_a, sin_b) * (HEAD_DIM ** -0.5)


def _sample_pre(x, past, cw_in, w_conv, cw_out, cg, cb, aw_in, tables):
    n = x.shape[0]
    full = lambda shape: pl.BlockSpec(shape, lambda i: (0,) * len(shape))
    weight = lambda shape: pl.BlockSpec(shape, lambda i: (0, 0), pipeline_mode=pl.Buffered(1))
    row = jax.ShapeDtypeStruct((n, D_MODEL), F32)
    return pl.pallas_call(
        _sample_pre_kernel,
        grid=(1,),
        in_specs=[
            full((n, D_MODEL)), full((n, 2 * D_MODEL)),
            weight((D_MODEL, 4 * D_MODEL)), full((CONV_WIDTH, D_MODEL)), weight((D_MODEL, D_MODEL)),
            full((1, D_MODEL)), full((1, D_MODEL)),
            weight((D_MODEL, 4 * D_MODEL)),
            full((1, LANES)), full((1, LANES)), full((1, LANES)),
        ],
        out_specs=[full((n, 2 * D_MODEL))] + [full((n, D_MODEL))] * 5,
        out_shape=[jax.ShapeDtypeStruct((n, 2 * D_MODEL), F32)] + [row] * 5,
        compiler_params=pltpu.CompilerParams(
            dimension_semantics=("arbitrary",), vmem_limit_bytes=VMEM_LIMIT_BYTES),
        name="sample_pre",
    )(x, past, cw_in, w_conv, cw_out, cg, cb, aw_in, *tables)


def _sample_decode_kernel(conv_every, rows_per_batch, pt_ref, q_ref, knew_ref, vnew_ref, lamv_ref,
                          x_ref, cw_in_ref, w_conv_ref, cw_out_ref, cg_ref, cb_ref,
                          cache_kt_ref, cache_v_ref,
                          o_ref, y1_ref, state_ref,
                          kt_ring, v_ring, sems, qm_ref, m_ref, l_ref, acc_ref, ubuf_ref):
    n = PAGES_PER_STEP
    j = pl.program_id(1)
    n_j = pl.num_programs(1)
    g = pl.program_id(0) * n_j + j
    n_steps = pl.num_programs(0) * n_j
    n_maps = 2 * N_HEADS
    rows = lax.broadcasted_iota(jnp.int32, (n_maps, D_MODEL), 0)
    lanes = lax.broadcasted_iota(jnp.int32, (n_maps, D_MODEL), 1)

    def page_copies(step, slot):
        copies = []
        for i in range(n):
            page = pt_ref[step * n + i]
            copies.append(pltpu.make_async_copy(
                cache_kt_ref.at[page], kt_ring.at[slot * n + i], sems.at[0, slot]))
            copies.append(pltpu.make_async_copy(
                cache_v_ref.at[page], v_ring.at[slot * n + i], sems.at[1, slot]))
        return copies

    @pl.when(g == 0)
    def _():
        for step in range(DEC_RING_STEPS):
            for c in page_copies(step, step):
                c.start()

    @pl.when(j % conv_every == 0)
    def _():
        first_tile = jnp.logical_and(j == 0, pl.program_id(0) % rows_per_batch == 0)
        _conv_prompt_tile(first_tile, x_ref, cw_in_ref, w_conv_ref, cw_out_ref, cg_ref, cb_ref,
                          y1_ref, state_ref, ubuf_ref)

    slot = g % DEC_RING_STEPS
    for c in page_copies(g, slot):
        c.wait()
    kt_pages = [kt_ring.at[slot * n + i] for i in range(n)]
    v_pages = [v_ring.at[slot * n + i] for i in range(n)]

    @pl.when(j == 0)
    def _():
        qm_ref[...] = jnp.where(lanes // HEAD_DIM == rows, q_ref[...], 0.0).astype(BF16)
        m_ref[...] = jnp.full(m_ref.shape, NEG_INF, F32)
        l_ref[...] = jnp.zeros(l_ref.shape, F32)
        acc_ref[...] = jnp.zeros(acc_ref.shape, F32)

    qm = qm_ref[...]
    s = jnp.concatenate([_dot(qm, kt_pages[i][...].astype(BF16)) for i in range(n)], axis=1)
    m_prev = m_ref[...]
    m_new = jnp.maximum(m_prev, jnp.max(s, axis=-1, keepdims=True))
    corr = jnp.exp(m_prev - m_new)
    p = jnp.exp(s - m_new)
    l_new = l_ref[...] * corr + jnp.sum(p, axis=-1, keepdims=True)
    pb = p.astype(BF16)

    def page_values(i):
        return jnp.concatenate(
            [v_pages[i][pl.ds(h, PAGE_SIZE, stride=N_HEADS), :] for h in range(N_HEADS)],
            axis=1).astype(BF16)

    pv = _dot(pb[:, 0:PAGE_SIZE], page_values(0))
    for i in range(1, n):
        pv += _dot(pb[:, i * PAGE_SIZE:(i + 1) * PAGE_SIZE], page_values(i))
    acc_new = acc_ref[...] * corr + pv
    m_ref[...] = m_new
    l_ref[...] = l_new
    acc_ref[...] = acc_new

    @pl.when(g + DEC_RING_STEPS < n_steps)
    def _():
        for c in page_copies(g + DEC_RING_STEPS, slot):
            c.start()

    @pl.when(j == pl.num_programs(1) - 1)
    def _():
        knew = knew_ref[...].astype(BF16).astype(F32)
        vnew = vnew_ref[...].astype(BF16).astype(F32)
        s_self = jnp.sum(qm.astype(F32) * knew, axis=-1, keepdims=True)
        m_fin = jnp.maximum(m_new, s_self)
        corr_f = jnp.exp(m_new - m_fin)
        p_self = jnp.exp(s_self - m_fin)
        l_fin = l_new * corr_f + p_self
        acc_fin = acc_new * corr_f + p_self.astype(BF16).astype(F32) * vnew
        o = acc_fin / l_fin
        lam = _lambda_full(lamv_ref[...])
        own_head = lanes // V_DIM == rows // 2
        weight = jnp.where(rows % 2 == 0, 1.0, -lam)
        o_ref[...] = jnp.sum(jnp.where(own_head, o * weight, 0.0), axis=0, keepdims=True)


def _sample_decode_conv(page_table, q, k_new, v_new, lamv, cache_kt, cache_v,
                        x, cw_in, w_conv, cw_out, cg, cb):
    n = q.shape[0]
    bsz, seq, _ = x.shape
    n_groups = page_table.shape[1] // PAGES_PER_STEP
    t = CONV_TILE
    n_tiles = bsz * seq // t
    conv_every = n * n_groups // n_tiles
    assert conv_every * n_tiles == n * n_groups and n_groups % conv_every == 0
    tiles_per_row = n_groups // conv_every
    assert (seq // t) % tiles_per_row == 0
    rows_per_batch = (seq // t) // tiles_per_row
    assert n * n_groups >= DEC_RING_STEPS

    pt = page_table.reshape(-1)
    const = lambda b, j, pt_ref: (0, 0)
    row = pl.BlockSpec((None, 1, D_MODEL), lambda b, j, pt_ref: (b, 0, 0))
    weight = lambda shape: pl.BlockSpec(shape, const, pipeline_mode=pl.Buffered(1))
    tile_index = lambda b, j, pt_ref: (b // rows_per_batch,
                                       (b % rows_per_batch) * tiles_per_row + j // conv_every, 0)
    hbm = pl.BlockSpec(memory_space=pl.ANY)

    grid_spec = pltpu.PrefetchScalarGridSpec(
        num_scalar_prefetch=1,
        grid=(n, n_groups),
        in_specs=[
            row, row, row, pl.BlockSpec((4, HEAD_DIM), const),
            pl.BlockSpec((1, t, D_MODEL), tile_index),
            weight((D_MODEL, 4 * D_MODEL)), pl.BlockSpec((CONV_WIDTH, D_MODEL), const),
            weight((D_MODEL, D_MODEL)),
            pl.BlockSpec((1, D_MODEL), const), pl.BlockSpec((1, D_MODEL), const),
            hbm, hbm,
        ],
        out_specs=[
            row,
            pl.BlockSpec((1, t, D_MODEL), tile_index),
            pl.BlockSpec((1, CONV_WIDTH - 1, D_MODEL), lambda b, j, pt_ref: (b // rows_per_batch, 0, 0)),
        ],
        scratch_shapes=[
            pltpu.VMEM((DEC_RING_STEPS * PAGES_PER_STEP, D_MODEL, PAGE_SIZE), F32),
            pltpu.VMEM((DEC_RING_STEPS * PAGES_PER_STEP, PAGE_SIZE * N_HEADS, V_DIM), F32),
            pltpu.SemaphoreType.DMA((2, DEC_RING_STEPS)),
            pltpu.VMEM((2 * N_HEADS, D_MODEL), BF16),
            pltpu.VMEM((2 * N_HEADS, 1), F32),
            pltpu.VMEM((2 * N_HEADS, 1), F32),
            pltpu.VMEM((2 * N_HEADS, D_MODEL), F32),
            pltpu.VMEM((t + SUBLANES, D_MODEL), F32),
        ],
    )
    r3 = lambda a: a.reshape(n, 1, D_MODEL)
    diff, y1, state = pl.pallas_call(
        functools.partial(_sample_decode_kernel, conv_every, rows_per_batch),
        grid_spec=grid_spec,
        out_shape=[
            jax.ShapeDtypeStruct((n, 1, D_MODEL), F32),
            jax.ShapeDtypeStruct((bsz, seq, D_MODEL), F32),
            jax.ShapeDtypeStruct((bsz, CONV_WIDTH - 1, D_MODEL), F32),
        ],
        compiler_params=pltpu.CompilerParams(
            dimension_semantics=("arbitrary", "arbitrary"),
            vmem_limit_bytes=VMEM_LIMIT_BYTES),
        name="sample_decode_conv",
    )(pt, r3(q), r3(k_new), r3(v_new), lamv, x, cw_in, w_conv, cw_out, cg, cb, cache_kt, cache_v)
    return diff.reshape(n, D_MODEL), y1, state


def _sample_post_kernel(diff_ref, z_ref, y1_ref, subln_ref, w_out_ref, g_ref, b_ref, y_ref):
    subln_g = subln_ref[...]
    of = jnp.concatenate(
        [_subln_gate(diff_ref[:, h * LANES:(h + 1) * LANES], subln_g) for h in range(N_HEADS)], axis=1)
    hout = _dot((of * _silu(z_ref[...])).astype(BF16), w_out_ref[...])
    y_ref[...] = _layer_norm(ALPHA * y1_ref[...] + hout, g_ref[...], b_ref[...])


def _sample_post(diff, z, y1, subln_g, w_out, g, b):
    n = diff.shape[0]
    full = lambda shape: pl.BlockSpec(shape, lambda i: (0,) * len(shape))
    return pl.pallas_call(
        _sample_post_kernel,
        grid=(1,),
        in_specs=[full((n, D_MODEL))] * 3
        + [full((1, V_DIM)), full((D_MODEL, D_MODEL)), full((1, D_MODEL)), full((1, D_MODEL))],
        out_specs=full((n, D_MODEL)),
        out_shape=jax.ShapeDtypeStruct((n, D_MODEL), F32),
        compiler_params=pltpu.CompilerParams(dimension_semantics=("arbitrary",)),
        name="sample_post",
    )(diff, z, y1, subln_g, w_out, g, b)


def kernel(x_prompt, x_sample, state_conv, cache_k, cache_v, page_table, conv_w_in, conv_w, conv_w_out,
           ln_conv_g, ln_conv_b, attn_w_in, lambda_q1, lambda_k1, lambda_q2, lambda_k2, subln_g,
           attn_w_out, ln_attn_g, ln_attn_b):
    bsz, seq, _ = x_prompt.shape
    n_dec = x_sample.shape[0]
    pool = cache_k.shape[0] * cache_k.shape[1]

    cw_in = conv_w_in[0].astype(BF16)
    cw_out = conv_w_out[0].astype(BF16)
    aw_in = attn_w_in[0].astype(BF16)
    aw_qvz = jnp.concatenate([aw_in[:, 0:D_MODEL], aw_in[:, 2 * D_MODEL:4 * D_MODEL]], axis=1)
    aw_kt = aw_in[:, D_MODEL:2 * D_MODEL].T
    aw_out = attn_w_out[0].astype(BF16)
    w_conv = conv_w[0]
    cg, cb = ln_conv_g[0:1], ln_conv_b[0:1]
    ag, ab = ln_attn_g[0:1], ln_attn_b[0:1]
    lamv = jnp.concatenate(
        [lambda_q1[0:1], lambda_k1[0:1], lambda_q2[0:1], lambda_k2[0:1]], axis=0)
    sg = subln_g[0:1]
    pos_p = jnp.arange(seq)
    tables_p = _rope_tables(pos_p)
    tables_pt = tuple(a.T for a in _rope_angles(pos_p))
    tables_s = _rope_tables(PAST_LEN + jnp.arange(1))

    past = state_conv[0].reshape(n_dec, (CONV_WIDTH - 1) * D_MODEL)
    state_s, y1_s, q_s, k_s, v_s, z_s = _sample_pre(
        x_sample.reshape(n_dec, D_MODEL), past, cw_in, w_conv, cw_out, cg, cb, aw_in, tables_s)
    cache_kt = cache_k.transpose(0, 1, 3, 4, 5, 2).reshape(pool, D_MODEL, PAGE_SIZE)
    cache_vr = cache_v.reshape(pool, PAGE_SIZE * N_HEADS, V_DIM)
    diff_s, y1_p, conv_state_p = _sample_decode_conv(
        page_table, q_s, k_s, v_s, lamv, cache_kt, cache_vr, x_prompt, cw_in, w_conv, cw_out, cg, cb)
    y_s = _sample_post(diff_s, z_s, y1_s, sg, aw_out, ag, ab)

    kt_p, v_p, y_p = _attn_prompt(y1_p, aw_qvz, aw_kt, aw_out, tables_p, tables_pt, lamv, sg, ag, ab)
    k_p = kt_p.reshape(1, bsz, N_HEADS, 2, HEAD_DIM, seq).transpose(0, 1, 5, 2, 3, 4)

    return (
        y_p,
        y_s.reshape(n_dec, 1, D_MODEL),
        conv_state_p[None],
        state_s.reshape(1, n_dec, CONV_WIDTH - 1, D_MODEL),
        k_p,
        v_p.reshape(1, bsz, seq, N_HEADS, V_DIM),
        k_s.reshape(1, n_dec, 1, N_HEADS, 2, HEAD_DIM),
        v_s.reshape(1, n_dec, 1, N_HEADS, V_DIM),
    )
```

```python
import functools
import math

import jax
import jax.numpy as jnp
from jax import lax
from jax.experimental import pallas as pl
from jax.experimental.pallas import tpu as pltpu

D_MODEL = 1024
DEPTH = 2
PAST_LEN = 16384
PAGE_SIZE = 128
CONV_WIDTH = 3
N_HEADS = 8
HEAD_DIM = D_MODEL // (2 * N_HEADS)
V_DIM = 2 * HEAD_DIM
ROT_DIM = HEAD_DIM // 4
ROT_HALF = ROT_DIM // 2
ROPE_THETA = 500000.0
LN_EPS = 1e-5
RMS_EPS = 1e-5
ALPHA = (2 * DEPTH) ** 0.25
NEG_INF = -1e30
LAM_INIT = 0.8 - 0.6 * math.exp(-0.3 * 1)
LOG2E = math.log2(math.e)

LANES = 128
SUBLANES = 8
VMEM_LIMIT_BYTES = 56 * 1024 * 1024

CONV_TILE = 256
ATTN_TILE = 512
ATTN_SUB = 256
PAGES_PER_STEP = 8
DEC_RING_STEPS = 3

F32 = jnp.float32
BF16 = jnp.bfloat16

assert ROT_HALF == SUBLANES


def _dot(a, b):
    return jnp.dot(a, b, preferred_element_type=F32)


def _dot_nt(a, b):
    return lax.dot_general(a, b, (((1,), (1,)), ((), ())), preferred_element_type=F32)


def _layer_norm(x, g, b):
    mu = jnp.mean(x, axis=-1, keepdims=True)
    xc = x - mu
    var = jnp.mean(xc * xc, axis=-1, keepdims=True)
    return xc * lax.rsqrt(var + LN_EPS) * g + b


def _silu(z):
    return z * (1.0 / (1.0 + jnp.exp(-z)))


def _rope_head(x, cos_t, sin_a, sin_b):
    up = pltpu.roll(x, LANES - ROT_HALF, axis=1)
    dn = pltpu.roll(x, ROT_HALF, axis=1)
    return x * cos_t + up * sin_a + dn * sin_b


def _lambda_full(lamv):
    a = jnp.sum(lamv[0:1, :] * lamv[1:2, :], axis=-1, keepdims=True)
    b = jnp.sum(lamv[2:3, :] * lamv[3:4, :], axis=-1, keepdims=True)
    return jnp.exp(a) - jnp.exp(b) + LAM_INIT


def _subln_gate(diff, subln_g):
    ms = jnp.mean(diff * diff, axis=-1, keepdims=True)
    return diff * lax.rsqrt(ms + RMS_EPS) * subln_g * (1.0 - LAM_INIT)


def _rope_angles(pos):
    inv_freq = jnp.power(ROPE_THETA, -jnp.arange(ROT_HALF, dtype=F32) * (2.0 / ROT_DIM))
    ang = pos.astype(F32)[:, None] * inv_freq[None, :]
    return jnp.cos(ang), jnp.sin(ang)


def _rope_tables(pos):
    cos, sin = _rope_angles(pos)
    rows = pos.shape[0]
    ones = jnp.ones((rows, HEAD_DIM - ROT_DIM), F32)
    zeros = jnp.zeros((rows, HEAD_DIM - ROT_DIM), F32)
    zh = jnp.zeros((rows, ROT_HALF), F32)
    cos64 = jnp.concatenate([cos, cos, ones], axis=1)
    sa64 = jnp.concatenate([-sin, zh, zeros], axis=1)
    sb64 = jnp.concatenate([zh, sin, zeros], axis=1)
    tile2 = lambda t: jnp.concatenate([t, t], axis=1)
    return tile2(cos64), tile2(sa64), tile2(sb64)


def _conv_prompt_tile(first_tile, x_ref, w_in_ref, w_conv_ref, w_out_ref, g_ref, b_ref,
                      y_ref, state_ref, ubuf_ref):
    t = x_ref.shape[1]
    pad = SUBLANES

    @pl.when(first_tile)
    def _():
        ubuf_ref[0:pad, :] = jnp.zeros((pad, D_MODEL), F32)

    x = x_ref[0]
    xb = x.astype(BF16)
    c = _dot(xb, w_in_ref[:, D_MODEL:2 * D_MODEL])
    v = _dot(xb, w_in_ref[:, 2 * D_MODEL:3 * D_MODEL])
    u = c * v
    ubuf_ref[pad:pad + t, :] = u
    conv = (w_conv_ref[0:1, :] * ubuf_ref[pad - 2:pad - 2 + t, :]
            + w_conv_ref[1:2, :] * ubuf_ref[pad - 1:pad - 1 + t, :]
            + w_conv_ref[2:3, :] * u)
    last2 = u[t - 2:t, :]
    ubuf_ref[pad - 2:pad, :] = last2
    state_ref[0] = last2
    bgate = _dot(xb, w_in_ref[:, 0:D_MODEL])
    z = _dot(xb, w_in_ref[:, 3 * D_MODEL:4 * D_MODEL])
    h = _dot((_silu(z) * bgate * conv).astype(BF16), w_out_ref[...])
    y_ref[0] = _layer_norm(ALPHA * x + h, g_ref[...], b_ref[...])


def _attn_prompt_kernel(x_ref, w_qvz_ref, w_kt_ref, w_out_ref, cos_ref, sa_ref, sb_ref,
                        cos_t_ref, sin_t_ref, lamv_ref, subln_ref, g_ref, b_ref,
                        kt_out_ref, v_out_ref, y_ref,
                        kc_ref, vc_ref, qs_ref, of_ref, m_ref, acc_ref):
    t = x_ref.shape[1]
    sub = ATTN_SUB
    n_sub = t // sub
    qi = pl.program_id(1)

    x = x_ref[0]
    xb = x.astype(BF16)

    kt = _dot_nt(w_kt_ref[...], xb)
    cos_c, sin_c = cos_t_ref[...], sin_t_ref[...]
    for h in range(N_HEADS):
        pieces = []
        for c in range(2):
            r0 = h * LANES + c * HEAD_DIM
            x1 = kt[r0:r0 + ROT_HALF, :]
            x2 = kt[r0 + ROT_HALF:r0 + ROT_DIM, :]
            pieces += [x1 * cos_c - x2 * sin_c, x2 * cos_c + x1 * sin_c,
                       kt[r0 + ROT_DIM:r0 + HEAD_DIM, :]]
        kth = jnp.concatenate(pieces, axis=0)
        kt_out_ref[0, h * LANES:(h + 1) * LANES, :] = kth
        for a in range(n_sub):
            kc_ref[h, qi * n_sub + a] = kth[:, a * sub:(a + 1) * sub].astype(BF16)

    v = _dot(xb, w_qvz_ref[:, D_MODEL:2 * D_MODEL])
    for h in range(N_HEADS):
        vh = v[:, h * LANES:(h + 1) * LANES]
        v_out_ref[0, pl.ds(h, t, stride=N_HEADS), :] = vh
        for a in range(n_sub):
            vc_ref[h, qi * n_sub + a] = vh[a * sub:(a + 1) * sub, :].astype(BF16)

    q = _dot(xb, w_qvz_ref[:, 0:D_MODEL])
    cos_r, sin_a, sin_b = cos_ref[...], sa_ref[...], sb_ref[...]
    first_map = lax.broadcasted_iota(jnp.int32, (t, LANES), 1) < HEAD_DIM
    for h in range(N_HEADS):
        qh = _rope_head(q[:, h * LANES:(h + 1) * LANES], cos_r, sin_a, sin_b) * (HEAD_DIM ** -0.5 * LOG2E)
        q1 = jnp.where(first_map, qh, 0.0).astype(BF16)
        q2 = jnp.where(first_map, 0.0, qh).astype(BF16)
        for a in range(n_sub):
            rs = slice(a * sub, (a + 1) * sub)
            qs_ref[h * n_sub + a, 0:sub, :] = q1[rs]
            qs_ref[h * n_sub + a, sub:2 * sub, :] = q2[rs]

    lam = _lambda_full(lamv_ref[...])
    subln_g = subln_ref[...]
    rows2 = lax.broadcasted_iota(jnp.int32, (2 * sub, sub), 0)
    cols2 = lax.broadcasted_iota(jnp.int32, (2 * sub, sub), 1)
    causal = cols2 <= jnp.where(rows2 >= sub, rows2 - sub, rows2)

    ones_cols = jnp.ones((sub, LANES), BF16)

    def block_update(a, h, kb, mask):
        s = _dot(qs_ref[h * n_sub + a], kc_ref[h, kb])
        if mask is not None:
            s = jnp.where(mask, s, NEG_INF)
        m_prev = m_ref[h]
        m_new = jnp.maximum(m_prev, jnp.max(s, axis=-1, keepdims=True))
        corr = jnp.exp2(m_prev - m_new)
        p = jnp.exp2(s - jnp.concatenate([m_new] * (sub // LANES), axis=1)).astype(BF16)
        v_ext = jnp.concatenate([vc_ref[h, kb], ones_cols], axis=1)
        acc_ref[h] = acc_ref[h] * jnp.concatenate([corr, corr], axis=1) + _dot(p, v_ext)
        m_ref[h] = m_new

    for a in range(n_sub):
        m_ref[...] = jnp.full(m_ref.shape, NEG_INF, F32)
        acc_ref[...] = jnp.zeros(acc_ref.shape, F32)
        n_full = qi * n_sub + a

        def kv_body(kb, c, a=a):
            for h in range(N_HEADS):
                block_update(a, h, kb, None)
            return c

        lax.fori_loop(0, n_full, kv_body, 0)
        for h in range(N_HEADS):
            block_update(a, h, n_full, causal)
        for h in range(N_HEADS):
            acc = acc_ref[h]
            o = acc[:, 0:LANES] / acc[:, LANES:2 * LANES]
            diff = o[0:sub, :] - lam * o[sub:2 * sub, :]
            of_ref[a * sub:(a + 1) * sub, h * LANES:(h + 1) * LANES] = _subln_gate(diff, subln_g)

    z = _dot(xb, w_qvz_ref[:, 2 * D_MODEL:3 * D_MODEL])
    hout = _dot((of_ref[...] * _silu(z)).astype(BF16), w_out_ref[...])
    y_ref[0] = _layer_norm(ALPHA * x + hout, g_ref[...], b_ref[...])


def _attn_prompt(x, w_qvz, w_kt, w_out, tables, tables_t, lamv, subln_g, g, b):
    bsz, s, _ = x.shape
    t = ATTN_TILE
    sub = ATTN_SUB
    const = lambda *_: (0, 0)
    weight = lambda shape: pl.BlockSpec(shape, const, pipeline_mode=pl.Buffered(1))
    tile = pl.BlockSpec((1, t, D_MODEL), lambda i, j: (i, j, 0))
    table = pl.BlockSpec((t, LANES), lambda i, j: (j, 0))
    table_t = pl.BlockSpec((ROT_HALF, t), lambda i, j: (0, j))
    return pl.pallas_call(
        _attn_prompt_kernel,
        grid=(bsz, s // t),
        in_specs=[
            tile,
            weight((D_MODEL, 3 * D_MODEL)), weight((D_MODEL, D_MODEL)), weight((D_MODEL, D_MODEL)),
            table, table, table, table_t, table_t,
            pl.BlockSpec((4, HEAD_DIM), const),
            pl.BlockSpec((1, V_DIM), const),
            pl.BlockSpec((1, D_MODEL), const),
            pl.BlockSpec((1, D_MODEL), const),
        ],
        out_specs=[
            pl.BlockSpec((1, D_MODEL, t), lambda i, j: (i, 0, j)),
            pl.BlockSpec((1, t * N_HEADS, V_DIM), lambda i, j: (i, j, 0)),
            tile,
        ],
        out_shape=[
            jax.ShapeDtypeStruct((bsz, D_MODEL, s), F32),
            jax.ShapeDtypeStruct((bsz, s * N_HEADS, V_DIM), F32),
            jax.ShapeDtypeStruct((bsz, s, D_MODEL), F32),
        ],
        scratch_shapes=[
            pltpu.VMEM((N_HEADS, s // sub, LANES, sub), BF16),
            pltpu.VMEM((N_HEADS, s // sub, sub, LANES), BF16),
            pltpu.VMEM((N_HEADS * (t // sub), 2 * sub, LANES), BF16),
            pltpu.VMEM((t, D_MODEL), F32),
            pltpu.VMEM((N_HEADS, 2 * sub, LANES), F32),
            pltpu.VMEM((N_HEADS, 2 * sub, 2 * LANES), F32),
        ],
        compiler_params=pltpu.CompilerParams(
            dimension_semantics=("arbitrary", "arbitrary"),
            vmem_limit_bytes=VMEM_LIMIT_BYTES),
        name="attn_prompt",
    )(x, w_qvz, w_kt, w_out, *tables, *tables_t, lamv, subln_g, g, b)


def _sample_pre_kernel(x_ref, past_ref, cw_in_ref, w_conv_ref, cw_out_ref, cg_ref, cb_ref,
                       aw_in_ref, cos_ref, sa_ref, sb_ref,
                       state_ref, y1_ref, q_ref, k_ref, v_ref, z_ref):
    x = x_ref[...]
    xb = x.astype(BF16)
    past0 = past_ref[:, 0:D_MODEL]
    past1 = past_ref[:, D_MODEL:2 * D_MODEL]
    bgate = _dot(xb, cw_in_ref[:, 0:D_MODEL])
    c = _dot(xb, cw_in_ref[:, D_MODEL:2 * D_MODEL])
    v = _dot(xb, cw_in_ref[:, 2 * D_MODEL:3 * D_MODEL])
    z = _dot(xb, cw_in_ref[:, 3 * D_MODEL:4 * D_MODEL])
    u = c * v
    conv = w_conv_ref[0:1, :] * past0 + w_conv_ref[1:2, :] * past1 + w_conv_ref[2:3, :] * u
    state_ref[:, 0:D_MODEL] = past1
    state_ref[:, D_MODEL:2 * D_MODEL] = u
    h = _dot((_silu(z) * bgate * conv).astype(BF16), cw_out_ref[...])
    y1 = _layer_norm(ALPHA * x + h, cg_ref[...], cb_ref[...])
    y1_ref[...] = y1

    yb = y1.astype(BF16)
    cos_t, sin_a, sin_b = cos_ref[...], sa_ref[...], sb_ref[...]
    q = _dot(yb, aw_in_ref[:, 0:D_MODEL])
    k = _dot(yb, aw_in_ref[:, D_MODEL:2 * D_MODEL])
    v_ref[...] = _dot(yb, aw_in_ref[:, 2 * D_MODEL:3 * D_MODEL])
    z_ref[...] = _dot(yb, aw_in_ref[:, 3 * D_MODEL:4 * D_MODEL])
    for h_i in range(N_HEADS):
        hs = slice(h_i * LANES, (h_i + 1) * LANES)
        k_ref[:, hs] = _rope_head(k[:, hs], cos_t, sin_a, sin_b)
        q_ref[:, hs] = _rope_head(q[:, hs], cos_t, sin_a, sin_b) * (HEAD_DIM ** -0.5)


def _sample_pre(x, past, cw_in, w_conv, cw_out, cg, cb, aw_in, tables):
    n = x.shape[0]
    full = lambda shape: pl.BlockSpec(shape, lambda i: (0,) * len(shape))
    weight = lambda shape: pl.BlockSpec(shape, lambda i: (0, 0), pipeline_mode=pl.Buffered(1))
    row = jax.ShapeDtypeStruct((n, D_MODEL), F32)
    return pl.pallas_call(
        _sample_pre_kernel,
        grid=(1,),
        in_specs=[
            full((n, D_MODEL)), full((n, 2 * D_MODEL)),
            weight((D_MODEL, 4 * D_MODEL)), full((CONV_WIDTH, D_MODEL)), weight((D_MODEL, D_MODEL)),
            full((1, D_MODEL)), full((1, D_MODEL)),
            weight((D_MODEL, 4 * D_MODEL)),
            full((1, LANES)), full((1, LANES)), full((1, LANES)),
        ],
        out_specs=[full((n, 2 * D_MODEL))] + [full((n, D_MODEL))] * 5,
        out_shape=[jax.ShapeDtypeStruct((n, 2 * D_MODEL), F32)] + [row] * 5,
        compiler_params=pltpu.CompilerParams(
            dimension_semantics=("arbitrary",), vmem_limit_bytes=VMEM_LIMIT_BYTES),
        name="sample_pre",
    )(x, past, cw_in, w_conv, cw_out, cg, cb, aw_in, *tables)


def _sample_decode_kernel(conv_every, tiles_per_batch, pt_ref, q_ref, knew_ref, vnew_ref, lamv_ref,
                          x_ref, cw_in_ref, w_conv_ref, cw_out_ref, cg_ref, cb_ref,
                          cache_kt_ref, cache_v_ref,
                          o_ref, y1_ref, state_ref,
                          kt_ring, v_ring, sems, qm_ref, m_ref, l_ref, acc_ref, ubuf_ref):
    n = PAGES_PER_STEP
    j = pl.program_id(1)
    n_j = pl.num_programs(1)
    g = pl.program_id(0) * n_j + j
    n_steps = pl.num_programs(0) * n_j
    n_maps = 2 * N_HEADS
    rows = lax.broadcasted_iota(jnp.int32, (n_maps, D_MODEL), 0)
    lanes = lax.broadcasted_iota(jnp.int32, (n_maps, D_MODEL), 1)

    def page_copies(step, slot):
        copies = []
        for i in range(n):
            page = pt_ref[step * n + i]
            copies.append(pltpu.make_async_copy(
                cache_kt_ref.at[page], kt_ring.at[slot * n + i], sems.at[0, slot]))
            copies.append(pltpu.make_async_copy(
                cache_v_ref.at[page], v_ring.at[slot * n + i], sems.at[1, slot]))
        return copies

    @pl.when(g == 0)
    def _():
        for step in range(DEC_RING_STEPS):
            for c in page_copies(step, step):
                c.start()

    @pl.when(g % conv_every == 0)
    def _():
        first_tile = (g // conv_every) % tiles_per_batch == 0
        _conv_prompt_tile(first_tile, x_ref, cw_in_ref, w_conv_ref, cw_out_ref, cg_ref, cb_ref,
                          y1_ref, state_ref, ubuf_ref)

    slot = g % DEC_RING_STEPS
    for c in page_copies(g, slot):
        c.wait()
    kt_pages = [kt_ring.at[slot * n + i] for i in range(n)]
    v_pages = [v_ring.at[slot * n + i] for i in range(n)]

    @pl.when(j == 0)
    def _():
        qm_ref[...] = jnp.where(lanes // HEAD_DIM == rows, q_ref[...], 0.0).astype(BF16)
        m_ref[...] = jnp.full(m_ref.shape, NEG_INF, F32)
        l_ref[...] = jnp.zeros(l_ref.shape, F32)
        acc_ref[...] = jnp.zeros(acc_ref.shape, F32)

    qm = qm_ref[...]
    s = jnp.concatenate([_dot(qm, kt_pages[i][...].astype(BF16)) for i in range(n)], axis=1)
    m_prev = m_ref[...]
    m_new = jnp.maximum(m_prev, jnp.max(s, axis=-1, keepdims=True))
    corr = jnp.exp(m_prev - m_new)
    p = jnp.exp(s - m_new)
    l_new = l_ref[...] * corr + jnp.sum(p, axis=-1, keepdims=True)
    pb = p.astype(BF16)

    def page_values(i):
        return jnp.concatenate(
            [v_pages[i][pl.ds(h, PAGE_SIZE, stride=N_HEADS), :] for h in range(N_HEADS)],
            axis=1).astype(BF16)

    pv = _dot(pb[:, 0:PAGE_SIZE], page_values(0))
    for i in range(1, n):
        pv += _dot(pb[:, i * PAGE_SIZE:(i + 1) * PAGE_SIZE], page_values(i))
    acc_new = acc_ref[...] * corr + pv
    m_ref[...] = m_new
    l_ref[...] = l_new
    acc_ref[...] = acc_new

    @pl.when(g + DEC_RING_STEPS < n_steps)
    def _():
        for c in page_copies(g + DEC_RING_STEPS, slot):
            c.start()

    @pl.when(j == pl.num_programs(1) - 1)
    def _():
        knew = knew_ref[...].astype(BF16).astype(F32)
        vnew = vnew_ref[...].astype(BF16).astype(F32)
        s_self = jnp.sum(qm.astype(F32) * knew, axis=-1, keepdims=True)
        m_fin = jnp.maximum(m_new, s_self)
        corr_f = jnp.exp(m_new - m_fin)
        p_self = jnp.exp(s_self - m_fin)
        l_fin = l_new * corr_f + p_self
        acc_fin = acc_new * corr_f + p_self.astype(BF16).astype(F32) * vnew
        o = acc_fin / l_fin
        lam = _lambda_full(lamv_ref[...])
        own_head = lanes // V_DIM == rows // 2
        weight = jnp.where(rows % 2 == 0, 1.0, -lam)
        o_ref[...] = jnp.sum(jnp.where(own_head, o * weight, 0.0), axis=0, keepdims=True)


def _sample_decode_conv(page_table, q, k_new, v_new, lamv, cache_kt, cache_v,
                        x, cw_in, w_conv, cw_out, cg, cb):
    n = q.shape[0]
    bsz, seq, _ = x.shape
    n_groups = page_table.shape[1] // PAGES_PER_STEP
    t = CONV_TILE
    n_tiles = bsz * seq // t
    conv_every = n * n_groups // n_tiles
    assert conv_every * n_tiles == n * n_groups and conv_every % 2 == 0
    tiles_per_batch = seq // t
    assert n * n_groups >= DEC_RING_STEPS

    pt = page_table.reshape(-1)
    const = lambda b, j, pt_ref: (0, 0)
    row = pl.BlockSpec((None, 1, D_MODEL), lambda b, j, pt_ref: (b, 0, 0))
    weight = lambda shape: pl.BlockSpec(shape, const, pipeline_mode=pl.Buffered(1))

    def tile_of_step(b, j):
        return jnp.minimum((b * n_groups + j + conv_every // 2) // conv_every, n_tiles - 1)

    def tile_index(b, j, pt_ref):
        c = tile_of_step(b, j)
        return (c // tiles_per_batch, c % tiles_per_batch, 0)

    hbm = pl.BlockSpec(memory_space=pl.ANY)

    grid_spec = pltpu.PrefetchScalarGridSpec(
        num_scalar_prefetch=1,
        grid=(n, n_groups),
        in_specs=[
            row, row, row, pl.BlockSpec((4, HEAD_DIM), const),
            pl.BlockSpec((1, t, D_MODEL), tile_index),
            weight((D_MODEL, 4 * D_MODEL)), pl.BlockSpec((CONV_WIDTH, D_MODEL), const),
            weight((D_MODEL, D_MODEL)),
            pl.BlockSpec((1, D_MODEL), const), pl.BlockSpec((1, D_MODEL), const),
            hbm, hbm,
        ],
        out_specs=[
            row,
            pl.BlockSpec((1, t, D_MODEL), tile_index),
            pl.BlockSpec((1, CONV_WIDTH - 1, D_MODEL),
                         lambda b, j, pt_ref: (tile_of_step(b, j) // tiles_per_batch, 0, 0)),
        ],
        scratch_shapes=[
            pltpu.VMEM((DEC_RING_STEPS * PAGES_PER_STEP, D_MODEL, PAGE_SIZE), F32),
            pltpu.VMEM((DEC_RING_STEPS * PAGES_PER_STEP, PAGE_SIZE * N_HEADS, V_DIM), F32),
            pltpu.SemaphoreType.DMA((2, DEC_RING_STEPS)),
            pltpu.VMEM((2 * N_HEADS, D_MODEL), BF16),
            pltpu.VMEM((2 * N_HEADS, 1), F32),
            pltpu.VMEM((2 * N_HEADS, 1), F32),
            pltpu.VMEM((2 * N_HEADS, D_MODEL), F32),
            pltpu.VMEM((t + SUBLANES, D_MODEL), F32),
        ],
    )
    r3 = lambda a: a.reshape(n, 1, D_MODEL)
    diff, y1, state = pl.pallas_call(
        functools.partial(_sample_decode_kernel, conv_every, tiles_per_batch),
        grid_spec=grid_spec,
        out_shape=[
            jax.ShapeDtypeStruct((n, 1, D_MODEL), F32),
            jax.ShapeDtypeStruct((bsz, seq, D_MODEL), F32),
            jax.ShapeDtypeStruct((bsz, CONV_WIDTH - 1, D_MODEL), F32),
        ],
        compiler_params=pltpu.CompilerParams(
            dimension_semantics=("arbitrary", "arbitrary"),
            vmem_limit_bytes=VMEM_LIMIT_BYTES),
        name="sample_decode_conv",
    )(pt, r3(q), r3(k_new), r3(v_new), lamv, x, cw_in, w_conv, cw_out, cg, cb, cache_kt, cache_v)
    return diff.reshape(n, D_MODEL), y1, state


def _sample_post_kernel(diff_ref, z_ref, y1_ref, subln_ref, w_out_ref, g_ref, b_ref, y_ref):
    subln_g = subln_ref[...]
    of = jnp.concatenate(
        [_subln_gate(diff_ref[:, h * LANES:(h + 1) * LANES], subln_g) for h in range(N_HEADS)], axis=1)
    hout = _dot((of * _silu(z_ref[...])).astype(BF16), w_out_ref[...])
    y_ref[...] = _layer_norm(ALPHA * y1_ref[...] + hout, g_ref[...], b_ref[...])


def _sample_post(diff, z, y1, subln_g, w_out, g, b):
    n = diff.shape[0]
    full = lambda shape: pl.BlockSpec(shape, lambda i: (0,) * len(shape))
    return pl.pallas_call(
        _sample_post_kernel,
        grid=(1,),
        in_specs=[full((n, D_MODEL))] * 3
        + [full((1, V_DIM)), full((D_MODEL, D_MODEL)), full((1, D_MODEL)), full((1, D_MODEL))],
        out_specs=full((n, D_MODEL)),
        out_shape=jax.ShapeDtypeStruct((n, D_MODEL), F32),
        compiler_params=pltpu.CompilerParams(dimension_semantics=("arbitrary",)),
        name="sample_post",
    )(diff, z, y1, subln_g, w_out, g, b)


def kernel(x_prompt, x_sample, state_conv, cache_k, cache_v, page_table, conv_w_in, conv_w, conv_w_out,
           ln_conv_g, ln_conv_b, attn_w_in, lambda_q1, lambda_k1, lambda_q2, lambda_k2, subln_g,
           attn_w_out, ln_attn_g, ln_attn_b):
    bsz, seq, _ = x_prompt.shape
    n_dec = x_sample.shape[0]
    pool = cache_k.shape[0] * cache_k.shape[1]

    cw_in = conv_w_in[0].astype(BF16)
    cw_out = conv_w_out[0].astype(BF16)
    aw_in = attn_w_in[0].astype(BF16)
    aw_qvz = jnp.concatenate([aw_in[:, 0:D_MODEL], aw_in[:, 2 * D_MODEL:4 * D_MODEL]], axis=1)
    aw_kt = aw_in[:, D_MODEL:2 * D_MODEL].T
    aw_out = attn_w_out[0].astype(BF16)
    w_conv = conv_w[0]
    cg, cb = ln_conv_g[0:1], ln_conv_b[0:1]
    ag, ab = ln_attn_g[0:1], ln_attn_b[0:1]
    lamv = jnp.concatenate(
        [lambda_q1[0:1], lambda_k1[0:1], lambda_q2[0:1], lambda_k2[0:1]], axis=0)
    sg = subln_g[0:1]
    pos_p = jnp.arange(seq)
    tables_p = _rope_tables(pos_p)
    tables_pt = tuple(a.T for a in _rope_angles(pos_p))
    tables_s = _rope_tables(PAST_LEN + jnp.arange(1))

    past = state_conv[0].reshape(n_dec, (CONV_WIDTH - 1) * D_MODEL)
    state_s, y1_s, q_s, k_s, v_s, z_s = _sample_pre(
        x_sample.reshape(n_dec, D_MODEL), past, cw_in, w_conv, cw_out, cg, cb, aw_in, tables_s)
    cache_kt = cache_k.transpose(0, 1, 3, 4, 5, 2).reshape(pool, D_MODEL, PAGE_SIZE)
    cache_vr = cache_v.reshape(pool, PAGE_SIZE * N_HEADS, V_DIM)
    diff_s, y1_p, conv_state_p = _sample_decode_conv(
        page_table, q_s, k_s, v_s, lamv, cache_kt, cache_vr, x_prompt, cw_in, w_conv, cw_out, cg, cb)
    y_s = _sample_post(diff_s, z_s, y1_s, sg, aw_out, ag, ab)

    kt_p, v_p, y_p = _attn_prompt(y1_p, aw_qvz, aw_kt, aw_out, tables_p, tables_pt, lamv, sg, ag, ab)
    k_p = kt_p.reshape(1, bsz, N_HEADS, 2, HEAD_DIM, seq).transpose(0, 1, 5, 2, 3, 4)

    return (
        y_p,
        y_s.reshape(n_dec, 1, D_MODEL),
        conv_state_p[None],
        state_s.reshape(1, n_dec, CONV_WIDTH - 1, D_MODEL),
        k_p,
        v_p.reshape(1, bsz, seq, N_HEADS, V_DIM),
        k_s.reshape(1, n_dec, 1, N_HEADS, 2, HEAD_DIM),
        v_s.reshape(1, n_dec, 1, N_HEADS, V_DIM),
    )
```
